```python
import math
import jax, jax.numpy as jnp
from jax import lax
import numpy as np

D_MODEL = 1024
BATCH = 4
SEQ = 8192
DEPTH = 4

D_BRANCH = 512
M_HEADS = 4
M_HEAD_DIM = D_BRANCH // M_HEADS
M_CHUNK = 64
M_CONV = 4
R_BLOCKS = 8
R_BLOCK_DIM = D_BRANCH // R_BLOCKS
R_CONV = 4
R_C = 8.0
A_HEADS = 8
A_HEAD_DIM = D_BRANCH // A_HEADS
IDX_HEADS = 8
IDX_DIM = 64
TOPK_MAX = 256
Q_BLOCK = 128
ROPE_THETA = 10000.0
D_FF = 3 * D_MODEL
FFN_CONV = 3
EPS = 1e-6

IN_SPLITS = (
    2 * D_BRANCH,
    D_BRANCH,
    D_BRANCH,
    M_HEADS,
    M_HEADS,
    D_BRANCH,
    D_BRANCH,
    D_BRANCH,
    D_BRANCH,
    D_BRANCH,
    IDX_HEADS * IDX_DIM,
    IDX_DIM,
    IDX_HEADS,
    3 * D_MODEL,
)
N_IN = sum(IN_SPLITS)

kernel_name = 'hybrid_mlstm_rglru_dsa_trunk'


def rms_norm(x, g):
    xf = x.astype(jnp.float32)
    y = xf * lax.rsqrt(jnp.mean(xf * xf, axis=-1, keepdims=True) + EPS)
    return (y * g.astype(jnp.float32)).astype(x.dtype)


def causal_dwconv(x, w, b):
    K, C = w.shape
    y = lax.conv_general_dilated(x, w[:, None, :].astype(x.dtype), window_strides=(1,),
                                 padding=[(K - 1, 0)], dimension_numbers=('NWC', 'WIO', 'NWC'),
                                 feature_group_count=C)
    return y + b.astype(x.dtype)


def rope_tables(positions, dim):
    inv = ROPE_THETA ** (-jnp.arange(0, dim, 2, dtype=jnp.float32) / dim)
    ang = positions.astype(jnp.float32)[..., None] * inv
    return jnp.cos(ang), jnp.sin(ang)


def apply_rope(x, cos, sin):
    x1, x2 = jnp.split(x.astype(jnp.float32), 2, axis=-1)
    c = cos[:, :, None, :]
    s = sin[:, :, None, :]
    return jnp.concatenate([x1 * c - x2 * s, x2 * c + x1 * s], axis=-1).astype(x.dtype)


def head_layer_norm(h, g):
    B, S, H, Dh = h.shape
    mu = jnp.mean(h, axis=-1, keepdims=True)
    var = jnp.mean(jnp.square(h - mu), axis=-1, keepdims=True)
    y = (h - mu) * lax.rsqrt(var + EPS)
    return y.reshape(B, S, H * Dh) * g.astype(jnp.float32)


def mlstm_chunkwise(q, k, v, i_pre, f_pre):
    B, S, H, Dh = q.shape
    L = M_CHUNK
    nc = S // L
    f32 = jnp.float32

    def to_chunks(t):
        return t.astype(f32).reshape(B, nc, L, H, Dh).transpose(1, 0, 3, 2, 4)

    def gate_chunks(t):
        return t.astype(f32).reshape(B, nc, L, H).transpose(1, 0, 3, 2)

    qc = to_chunks(q)
    kc = to_chunks(k) * (Dh ** -0.5)
    vc = to_chunks(v)
    ic = gate_chunks(i_pre)
    gc = jnp.cumsum(jax.nn.log_sigmoid(gate_chunks(f_pre)), axis=-1)
    causal = jnp.tril(jnp.ones((L, L), dtype=bool))

    def step(carry, inp):
        C, n, m = carry
        qb, kb, vb, ib, gb = inp
        logd = jnp.where(causal, gb[..., :, None] - gb[..., None, :] + ib[..., None, :], -jnp.inf)
        log_inter = gb + m[..., None]
        m_t = jnp.maximum(log_inter, jnp.max(logd, axis=-1))
        s = jnp.einsum('bhtd,bhsd->bhts', qb, kb) * jnp.exp(logd - m_t[..., None])
        w_inter = jnp.exp(log_inter - m_t)
        num = jnp.einsum('bhts,bhsd->bhtd', s, vb) + w_inter[..., None] * jnp.einsum('bhvk,bhtk->bhtv', C, qb)
        den = jnp.sum(s, axis=-1) + w_inter * jnp.einsum('bhk,bhtk->bht', n, qb)
        h = num / jnp.maximum(jnp.abs(den), jnp.exp(-m_t))[..., None]
        g_last = gb[..., -1]
        m_new = m_t[..., -1]
        w_state = jnp.exp(g_last[..., None] - gb + ib - m_new[..., None])
        decay = jnp.exp(g_last + m - m_new)
        C = decay[..., None, None] * C + jnp.einsum('bhs,bhsv,bhsk->bhvk', w_state, vb, kb)
        n = decay[..., None] * n + jnp.einsum('bhs,bhsk->bhk', w_state, kb)
        return (C, n, m_new), h

    init = (jnp.zeros((B, H, Dh, Dh), f32), jnp.zeros((B, H, Dh), f32), jnp.zeros((B, H), f32))
    _, hs = lax.scan(step, init, (qc, kc, vc, ic, gc))
    return hs.transpose(1, 0, 3, 2, 4).reshape(B, S, H, Dh)


def rg_lru(x, w_a, b_a, w_x, b_x, lam):
    B, S, C = x.shape
    xf = x.astype(jnp.float32)
    xb = xf.reshape(B, S, R_BLOCKS, R_BLOCK_DIM)
    r = jax.nn.sigmoid(jnp.einsum('bsnd,nde->bsne', xb, w_a.astype(jnp.float32)).reshape(B, S, C) + b_a.astype(jnp.float32))
    i = jax.nn.sigmoid(jnp.einsum('bsnd,nde->bsne', xb, w_x.astype(jnp.float32)).reshape(B, S, C) + b_x.astype(jnp.float32))
    log_a = -R_C * r * jax.nn.softplus(-lam.astype(jnp.float32))
    a = jnp.exp(log_a)
    u = jnp.sqrt(-jnp.expm1(2.0 * log_a)) * (i * xf)

    def combine(left, right):
        a1, b1 = left
        a2, b2 = right
        return a1 * a2, a2 * b1 + b2

    _, h = lax.associative_scan(combine, (a, u), axis=1)
    return h.astype(x.dtype)


def dsa_attention(q, k, v, q_idx, k_idx, w_idx):
    B, S, H, Dh = q.shape
    f32 = jnp.float32
    top_k = min(TOPK_MAX, S // 4)
    nb = S // Q_BLOCK
    key_pos = jnp.arange(S, dtype=jnp.int32)
    k_idx_f = k_idx.astype(f32)

    def blocks(t):
        return jnp.moveaxis(t.reshape((B, nb, Q_BLOCK) + t.shape[2:]), 1, 0)

    def attend(args):
        qb, qib, wb, start = args
        qpos = start + jnp.arange(Q_BLOCK, dtype=jnp.int32)
        visible = key_pos[None, None, :] <= qpos[None, :, None]
        logits = jnp.einsum('bqhd,bsd->bqhs', qib.astype(f32), k_idx_f) * (IDX_DIM ** -0.5)
        score = jnp.einsum('bqh,bqhs->bqs', wb.astype(f32) * (IDX_HEADS ** -0.5), jax.nn.relu(logits))
        score = jnp.where(visible, score, -jnp.inf)
        _, idx = lax.top_k(score, top_k)
        k_sel = jax.vmap(lambda kk, ii: kk[ii])(k, idx)
        v_sel = jax.vmap(lambda vv, ii: vv[ii])(v, idx)
        att = jnp.einsum('bqhd,bqkhd->bqhk', qb.astype(f32), k_sel.astype(f32)) * (Dh ** -0.5)
        keep = (idx <= qpos[None, :, None])[:, :, None, :]
        p = jax.nn.softmax(jnp.where(keep, att, -jnp.inf), axis=-1)
        return jnp.einsum('bqhk,bqkhd->bqhd', p, v_sel.astype(f32)).astype(q.dtype)

    starts = jnp.arange(nb, dtype=jnp.int32) * Q_BLOCK
    out = lax.map(attend, (blocks(q), blocks(q_idx), blocks(w_idx), starts))
    return jnp.moveaxis(out, 0, 1).reshape(B, S, H, Dh)


def setup_inputs(seed: int = 0) -> dict:
    key = jax.random.key(seed)
    ks = jax.random.split(key, 24)
    f32 = jnp.float32

    def normal(k, shape, scale):
        return jax.random.normal(k, shape, f32) * scale

    def gain(k, shape):
        return 1.0 + 0.02 * jax.random.normal(k, shape, f32)

    x = normal(ks[0], (BATCH, SEQ, D_MODEL), 1.0)
    positions = (jnp.arange(SEQ, dtype=jnp.int32)[None, :]
                 + jax.random.randint(ks[1], (BATCH, 1), 0, 4096, dtype=jnp.int32))
    norm_mix = gain(ks[2], (DEPTH, D_MODEL))
    w_in = normal(ks[3], (DEPTH, D_MODEL, N_IN), D_MODEL ** -0.5)
    mlstm_conv_w = normal(ks[4], (DEPTH, M_CONV, 2 * D_BRANCH), M_CONV ** -0.5)
    mlstm_conv_b = normal(ks[5], (DEPTH, 2 * D_BRANCH), 0.02)
    mlstm_i_bias = normal(ks[6], (DEPTH, M_HEADS), 0.1)
    mlstm_f_bias = jnp.linspace(3.0, 6.0, M_HEADS, dtype=f32)[None, :] + normal(ks[7], (DEPTH, M_HEADS), 0.1)
    mlstm_norm = gain(ks[8], (DEPTH, D_BRANCH))
    rglru_conv_w = normal(ks[9], (DEPTH, R_CONV, D_BRANCH), R_CONV ** -0.5)
    rglru_conv_b = normal(ks[10], (DEPTH, D_BRANCH), 0.02)
    rglru_w_a = normal(ks[11], (DEPTH, R_BLOCKS, R_BLOCK_DIM, R_BLOCK_DIM), R_BLOCK_DIM ** -0.5)
    rglru_b_a = normal(ks[12], (DEPTH, D_BRANCH), 0.02)
    rglru_w_x = normal(ks[13], (DEPTH, R_BLOCKS, R_BLOCK_DIM, R_BLOCK_DIM), R_BLOCK_DIM ** -0.5)
    rglru_b_x = normal(ks[14], (DEPTH, D_BRANCH), 0.02)
    a_c = jax.random.uniform(ks[15], (DEPTH, D_BRANCH), f32, 0.9, 0.999)
    sig = a_c ** (1.0 / R_C)
    rglru_lambda = jnp.log(sig) - jnp.log1p(-sig)
    w_branch = normal(ks[16], (DEPTH, 3, D_BRANCH, D_MODEL), D_BRANCH ** -0.5)
    w_out = normal(ks[17], (DEPTH, D_MODEL, D_MODEL), D_MODEL ** -0.5)
    norm_ffn = gain(ks[18], (DEPTH, D_MODEL))
    ffn_up = normal(ks[19], (DEPTH, D_MODEL, 2 * D_FF), D_MODEL ** -0.5)
    ffn_conv_w = normal(ks[20], (DEPTH, FFN_CONV, D_FF), FFN_CONV ** -0.5)
    ffn_conv_b = normal(ks[21], (DEPTH, D_FF), 0.02)
    ffn_down = normal(ks[22], (DEPTH, D_FF, D_MODEL), D_FF ** -0.5)
    norm_final = gain(ks[23], (D_MODEL,))
    return {'x': x, 'positions': positions, 'norm_mix': norm_mix, 'w_in': w_in,
            'mlstm_conv_w': mlstm_conv_w, 'mlstm_conv_b': mlstm_conv_b,
            'mlstm_i_bias': mlstm_i_bias, 'mlstm_f_bias': mlstm_f_bias, 'mlstm_norm': mlstm_norm,
            'rglru_conv_w': rglru_conv_w, 'rglru_conv_b': rglru_conv_b,
            'rglru_w_a': rglru_w_a, 'rglru_b_a': rglru_b_a, 'rglru_w_x': rglru_w_x, 'rglru_b_x': rglru_b_x,
            'rglru_lambda': rglru_lambda, 'w_branch': w_branch, 'w_out': w_out, 'norm_ffn': norm_ffn,
            'ffn_up': ffn_up, 'ffn_conv_w': ffn_conv_w, 'ffn_conv_b': ffn_conv_b, 'ffn_down': ffn_down,
            'norm_final': norm_final}


def reference(x, positions, norm_mix, w_in, mlstm_conv_w, mlstm_conv_b, mlstm_i_bias, mlstm_f_bias,
              mlstm_norm, rglru_conv_w, rglru_conv_b, rglru_w_a, rglru_b_a, rglru_w_x, rglru_b_x,
              rglru_lambda, w_branch, w_out, norm_ffn, ffn_up, ffn_conv_w, ffn_conv_b, ffn_down,
              norm_final):
    B, S, _ = x.shape
    cos, sin = rope_tables(positions, A_HEAD_DIM)
    split_at = np.cumsum(IN_SPLITS)[:-1].tolist()
    m_shape = (B, S, M_HEADS, M_HEAD_DIM)
    a_shape = (B, S, A_HEADS, A_HEAD_DIM)
    for l in range(DEPTH):
        h = rms_norm(x, norm_mix[l])
        (m_qk, m_v, m_o, m_i, m_f, r_x, r_g, a_q, a_k, a_v, a_qi, a_ki, a_w, gates) = jnp.split(
            h @ w_in[l], split_at, axis=-1)
        m_qk = jax.nn.silu(causal_dwconv(m_qk, mlstm_conv_w[l], mlstm_conv_b[l]))
        m_q, m_k = jnp.split(m_qk, 2, axis=-1)
        h_m = mlstm_chunkwise(m_q.reshape(m_shape), m_k.reshape(m_shape), m_v.reshape(m_shape),
                              m_i + mlstm_i_bias[l], m_f + mlstm_f_bias[l])
        y_a = (head_layer_norm(h_m, mlstm_norm[l]) * jax.nn.sigmoid(m_o.astype(jnp.float32))).astype(x.dtype)
        r_x = causal_dwconv(r_x, rglru_conv_w[l], rglru_conv_b[l])
        y_b = jax.nn.gelu(r_g) * rg_lru(r_x, rglru_w_a[l], rglru_b_a[l], rglru_w_x[l], rglru_b_x[l], rglru_lambda[l])
        q_c = apply_rope(a_q.reshape(a_shape), cos, sin)
        k_c = apply_rope(a_k.reshape(a_shape), cos, sin)
        qi = apply_rope(a_qi.reshape(B, S, IDX_HEADS, IDX_DIM), cos, sin)
        ki = apply_rope(a_ki[:, :, None, :], cos, sin)[:, :, 0, :]
        y_c = dsa_attention(q_c, k_c, a_v.reshape(a_shape), qi, ki, a_w).reshape(B, S, D_BRANCH)
        g_a, g_b, g_c = jnp.split(jax.nn.sigmoid(gates), 3, axis=-1)
        merged = (g_a * (y_a @ w_branch[l, 0]) + g_b * (y_b @ w_branch[l, 1])
                  + g_c * (y_c @ w_branch[l, 2]))
        x = x + merged @ w_out[l]
        h = rms_norm(x, norm_ffn[l])
        ff_g, ff_u = jnp.split(h @ ffn_up[l], 2, axis=-1)
        ff_g = causal_dwconv(ff_g, ffn_conv_w[l], ffn_conv_b[l])
        x = x + (jax.nn.gelu(ff_g) * ff_u) @ ffn_down[l]
    return rms_norm(x, norm_final)
```

```python
import functools

import jax
import jax.numpy as jnp
import numpy as np
from jax import lax
from jax.experimental import pallas as pl
from jax.experimental.pallas import tpu as pltpu

F32 = jnp.float32
BF16 = jnp.bfloat16
I32 = jnp.int32

EPS = 1e-6
ROPE_THETA = 10000.0
R_C = 8.0
TOPK_MAX = 256
INT_MIN = -(2 ** 31)
NEG_BIG = -1e30

LANES = 128
SUBLANES = 8
VMEM_LIMIT = 56 * 1024 * 1024

M_HEADS = 4
R_BLOCKS = 8
A_HEADS = 8
IDX_HEADS = 8
M_CONV = 4
R_CONV = 4
FFN_CONV = 3

MISC_KI = 0
MISC_I = 64
MISC_F = 68
MISC_W = 72


def _cparams(sem):
    return pltpu.CompilerParams(dimension_semantics=sem, vmem_limit_bytes=VMEM_LIMIT)


def _sigmoid(x):
    return 1.0 / (1.0 + jnp.exp(-x))


def _log_sigmoid(x):
    return jnp.minimum(x, 0.0) - jnp.log1p(jnp.exp(-jnp.abs(x)))


def _expm1(y):
    p = jnp.ones_like(y)
    for n in range(13, 1, -1):
        p = 1.0 + (y * (1.0 / n)) * p
    return jnp.where(jnp.abs(y) < 0.25, y * p, jnp.exp(y) - 1.0)


def _gelu(x):
    return 0.5 * x * (1.0 + jnp.tanh(0.7978845608028654 * (x + 0.044715 * (x * x * x))))


def _tile_lanes(x, n):
    return jnp.concatenate([x] * n, axis=1)


def _causal_conv(xs_ref, x, w_ref, b_ref, first, taps):
    tm = x.shape[0]

    @pl.when(first)
    def _():
        xs_ref[0:SUBLANES, :] = jnp.zeros((SUBLANES, x.shape[1]), F32)

    xs_ref[SUBLANES:SUBLANES + tm, :] = x
    y = b_ref[...] + w_ref[taps - 1:taps, :] * x
    for j in range(taps - 1):
        d = taps - 1 - j
        y = y + w_ref[j:j + 1, :] * xs_ref[SUBLANES - d:SUBLANES - d + tm, :]
    xs_ref[0:SUBLANES, :] = xs_ref[tm:tm + SUBLANES, :]
    return y


def _rope_table_kernel(pos_ref, inv_ref, sgn_ref, cos_ref, sin_ref):
    ang = pos_ref[...] * inv_ref[...]
    cos_ref[...] = jnp.cos(ang)
    sin_ref[...] = jnp.sin(ang) * sgn_ref[...]


def _rope_tables(positions, head_dim):
    T = positions.size
    half = head_dim // 2
    inv = ROPE_THETA ** (-jnp.arange(0, head_dim, 2, dtype=F32) / head_dim)
    lane = np.arange(LANES)
    inv_row = inv[lane % half][None, :]
    sgn_row = jnp.asarray(np.where((lane % head_dim) < half, -1.0, 1.0), F32)[None, :]
    pos = positions.reshape(T, 1).astype(F32)
    tm = min(T, 1024)
    return pl.pallas_call(
        _rope_table_kernel,
        grid=(T // tm,),
        in_specs=[pl.BlockSpec((tm, 1), lambda i: (i, 0)),
                  pl.BlockSpec((1, LANES), lambda i: (0, 0)),
                  pl.BlockSpec((1, LANES), lambda i: (0, 0))],
        out_specs=[pl.BlockSpec((tm, LANES), lambda i: (i, 0)),
                   pl.BlockSpec((tm, LANES), lambda i: (i, 0))],
        out_shape=[jax.ShapeDtypeStruct((T, LANES), F32)] * 2,
        compiler_params=_cparams(("parallel",)),
        name="rope_tables",
    )(pos, inv_row, sgn_row)


def _norm_matmul_kernel(x_ref, g_ref, w_ref, o_ref, h_ref):
    @pl.when(pl.program_id(1) == 0)
    def _():
        x = x_ref[...]
        ms = jnp.mean(x * x, axis=-1, keepdims=True)
        h_ref[...] = (x * lax.rsqrt(ms + EPS) * g_ref[...]).astype(BF16)

    o_ref[...] = jnp.dot(h_ref[...], w_ref[...], preferred_element_type=F32).astype(o_ref.dtype)


def _norm_matmul(x, g, w, out_dtype, tm, tn, name):
    T, D = x.shape
    N = w.shape[1]
    return pl.pallas_call(
        _norm_matmul_kernel,
        grid=(T // tm, N // tn),
        in_specs=[pl.BlockSpec((tm, D), lambda i, j: (i, 0)),
                  pl.BlockSpec((1, D), lambda i, j: (0, 0)),
                  pl.BlockSpec((D, tn), lambda i, j: (0, j))],
        out_specs=pl.BlockSpec((tm, tn), lambda i, j: (i, j)),
        out_shape=jax.ShapeDtypeStruct((T, N), out_dtype),
        scratch_shapes=[pltpu.VMEM((tm, D), BF16)],
        compiler_params=_cparams(("parallel", "arbitrary")),
        name=name,
    )(x, g, w)


def _row_cumsum(x):
    n = x.shape[0]
    row = lax.broadcasted_iota(I32, x.shape, 0)
    d = 1
    while d < n:
        x = x + jnp.where(row >= d, pltpu.roll(x, d, axis=0), 0.0)
        d *= 2
    return x


def _mlstm_kernel(qk_ref, v_ref, o_ref, misc_ref, cw_ref, cb_ref, gb_ref, gn_ref, out_ref,
                  xs_ref, ct_ref, n_ref, m_ref, *, heads, dh):
    first = pl.program_id(1) == 0

    @pl.when(first)
    def _():
        ct_ref[...] = jnp.zeros_like(ct_ref)
        n_ref[...] = jnp.zeros_like(n_ref)
        m_ref[...] = jnp.zeros_like(m_ref)

    L = qk_ref.shape[0]
    hd = heads * dh
    qk = _causal_conv(xs_ref, qk_ref[...].astype(F32), cw_ref, cb_ref, first, M_CONV)
    qk = qk * _sigmoid(qk)

    pre = misc_ref[...] + gb_ref[...]
    g_all = _row_cumsum(_log_sigmoid(pre))
    g_t = g_all.T
    pre_t = pre.T
    row = lax.broadcasted_iota(I32, (L, L), 0)
    col = lax.broadcasted_iota(I32, (L, L), 1)
    causal = row >= col

    for h in range(heads):
        q = qk[:, h * dh:(h + 1) * dh]
        k = qk[:, hd + h * dh:hd + (h + 1) * dh] * (dh ** -0.5)
        v = v_ref[:, h * dh:(h + 1) * dh]
        g_c = g_all[:, MISC_F + h:MISC_F + h + 1]
        i_c = pre[:, MISC_I + h:MISC_I + h + 1]
        g_r = g_t[MISC_F + h:MISC_F + h + 1, :]
        i_r = pre_t[MISC_I + h:MISC_I + h + 1, :]
        m_prev = m_ref[h:h + 1, 0:1]

        logd = jnp.where(causal, g_c - g_r + i_r, -jnp.inf)
        log_inter = g_c + m_prev
        m_t = jnp.maximum(log_inter, jnp.max(logd, axis=-1, keepdims=True))
        qb = q.astype(BF16)
        kb = k.astype(BF16)
        s = lax.dot_general(qb, kb, (((1,), (1,)), ((), ())), preferred_element_type=F32)
        s = s * jnp.exp(logd - m_t)
        w_inter = jnp.exp(log_inter - m_t)
        num = jnp.dot(s.astype(BF16), v, preferred_element_type=F32)
        num = num + w_inter * jnp.dot(qb, ct_ref[h].astype(BF16), preferred_element_type=F32)
        den = jnp.sum(s, axis=-1, keepdims=True)
        den = den + w_inter * jnp.sum(q * n_ref[h:h + 1, :], axis=-1, keepdims=True)
        hh = num / jnp.maximum(jnp.abs(den), jnp.exp(-m_t))

        g_last = g_c[L - 1:L, :]
        m_new = m_t[L - 1:L, :]
        w_state = jnp.exp(g_last - g_c + i_c - m_new)
        decay = jnp.exp(g_last + m_prev - m_new)
        kw = k * w_state
        ct_ref[h] = decay * ct_ref[h] + lax.dot_general(
            kw.astype(BF16), v, (((0,), (0,)), ((), ())), preferred_element_type=F32)
        n_ref[h:h + 1, :] = decay * n_ref[h:h + 1, :] + jnp.sum(kw, axis=0, keepdims=True)
        m_ref[h:h + 1, :] = jnp.broadcast_to(m_new, (1, LANES))

        mu = jnp.mean(hh, axis=-1, keepdims=True)
        hc = hh - mu
        var = jnp.mean(hc * hc, axis=-1, keepdims=True)
        y = hc * lax.rsqrt(var + EPS) * gn_ref[:, h * dh:(h + 1) * dh]
        y = y * _sigmoid(o_ref[:, h * dh:(h + 1) * dh].astype(F32))
        out_ref[:, h * dh:(h + 1) * dh] = y.astype(out_ref.dtype)


def _mlstm(proj, misc, conv_w, conv_b, gate_bias, norm_g, B, S, L):
    T = B * S
    nc = S // L
    hd = norm_g.shape[1]
    dh = hd // M_HEADS
    kern = functools.partial(_mlstm_kernel, heads=M_HEADS, dh=dh)
    return pl.pallas_call(
        kern,
        grid=(B, nc),
        in_specs=[pl.BlockSpec((L, 2 * hd), lambda b, c: (b * nc + c, 0)),
                  pl.BlockSpec((L, hd), lambda b, c: (b * nc + c, 2)),
                  pl.BlockSpec((L, hd), lambda b, c: (b * nc + c, 3)),
                  pl.BlockSpec((L, LANES), lambda b, c: (b * nc + c, 0)),
                  pl.BlockSpec((M_CONV, 2 * hd), lambda b, c: (0, 0)),
                  pl.BlockSpec((1, 2 * hd), lambda b, c: (0, 0)),
                  pl.BlockSpec((1, LANES), lambda b, c: (0, 0)),
                  pl.BlockSpec((1, hd), lambda b, c: (0, 0))],
        out_specs=pl.BlockSpec((L, hd), lambda b, c: (b * nc + c, 0)),
        out_shape=jax.ShapeDtypeStruct((T, hd), BF16),
        scratch_shapes=[pltpu.VMEM((L + SUBLANES, 2 * hd), F32),
                        pltpu.VMEM((M_HEADS, dh, dh), F32),
                        pltpu.VMEM((SUBLANES, dh), F32),
                        pltpu.VMEM((SUBLANES, LANES), F32)],
        compiler_params=_cparams(("parallel", "arbitrary")),
        name="mlstm",
    )(proj, proj, proj, misc, conv_w, conv_b, gate_bias, norm_g)


def _rglru_kernel(x_ref, g_ref, cw_ref, cb_ref, wa_ref, ba_ref, wx_ref, bx_ref, lam_ref, out_ref,
                  xs_ref, h_ref):
    first = pl.program_id(1) == 0

    @pl.when(first)
    def _():
        h_ref[...] = jnp.zeros_like(h_ref)

    tm = x_ref.shape[0]
    x = _causal_conv(xs_ref, x_ref[...].astype(F32), cw_ref, cb_ref, first, R_CONV)
    xb = x.astype(BF16)
    r = _sigmoid(jnp.dot(xb, wa_ref[...], preferred_element_type=F32) + ba_ref[...])
    i = _sigmoid(jnp.dot(xb, wx_ref[...], preferred_element_type=F32) + bx_ref[...])
    lam = lam_ref[...]
    softplus_neg_lam = jnp.maximum(-lam, 0.0) + jnp.log1p(jnp.exp(-jnp.abs(lam)))
    log_a = -R_C * r * softplus_neg_lam
    a = jnp.exp(log_a)
    u = jnp.sqrt(-_expm1(2.0 * log_a)) * (i * x)

    row = lax.broadcasted_iota(I32, a.shape, 0)
    d = 1
    while d < tm:
        valid = row >= d
        u = jnp.where(valid, a * pltpu.roll(u, d, axis=0) + u, u)
        a = jnp.where(valid, a * pltpu.roll(a, d, axis=0), a)
        d *= 2
    hcur = a * h_ref[0:1, :] + u
    h_ref[0:1, :] = hcur[tm - 1:tm, :]
    out_ref[...] = (_gelu(g_ref[...].astype(F32)) * hcur).astype(out_ref.dtype)


def _rglru(proj, conv_w, conv_b, wa, ba, wx, bx, lam, B, S, tm):
    T = B * S
    ns = S // tm
    C = lam.shape[1]
    vec = pl.BlockSpec((1, C), lambda b, s: (0, 0))
    mat = pl.BlockSpec((C, C), lambda b, s: (0, 0))
    return pl.pallas_call(
        _rglru_kernel,
        grid=(B, ns),
        in_specs=[pl.BlockSpec((tm, C), lambda b, s: (b * ns + s, 4)),
                  pl.BlockSpec((tm, C), lambda b, s: (b * ns + s, 5)),
                  pl.BlockSpec((R_CONV, C), lambda b, s: (0, 0)), vec,
                  mat, vec, mat, vec, vec],
        out_specs=pl.BlockSpec((tm, C), lambda b, s: (b * ns + s, 0)),
        out_shape=jax.ShapeDtypeStruct((T, C), BF16),
        scratch_shapes=[pltpu.VMEM((tm + SUBLANES, C), F32),
                        pltpu.VMEM((SUBLANES, C), F32)],
        compiler_params=_cparams(("parallel", "arbitrary")),
        name="rglru",
    )(proj, proj, conv_w, conv_b, wa, ba, wx, bx, lam)


def _rope_slab(x, cos, sin_signed, first_half):
    fwd = pltpu.roll(x, 32, axis=1)
    bwd = pltpu.roll(x, 96, axis=1)
    return x * cos + jnp.where(first_half, bwd, fwd) * sin_signed


def _dsa_prep_kernel(q_ref, k_ref, qi_ref, misc_ref, cos_ref, sin_ref,
                     qo_ref, ko_ref, qio_ref, kio_ref, *, q_scale, idx_scale, head_dim):
    tm = q_ref.shape[0]
    cos = cos_ref[...]
    sin = sin_ref[...]
    lane = lax.broadcasted_iota(I32, (tm, LANES), 1)
    first_half = (lane % head_dim) < (head_dim // 2)
    low = lane < head_dim
    misc = misc_ref[...]
    nslab = q_ref.shape[1] // LANES
    for s in range(nslab):
        sl = slice(s * LANES, (s + 1) * LANES)
        qo_ref[:, sl] = (_rope_slab(q_ref[:, sl].astype(F32), cos, sin, first_half) * q_scale).astype(BF16)
        ko_ref[:, sl] = _rope_slab(k_ref[:, sl].astype(F32), cos, sin, first_half).astype(BF16)
        w_even = jnp.abs(misc[:, MISC_W + 2 * s:MISC_W + 2 * s + 1])
        w_odd = jnp.abs(misc[:, MISC_W + 2 * s + 1:MISC_W + 2 * s + 2])
        w = jnp.where(low, w_even, w_odd) * idx_scale
        qio_ref[:, sl] = (_rope_slab(qi_ref[:, sl].astype(F32), cos, sin, first_half) * w).astype(BF16)
    ki = jnp.where(low, _rope_slab(misc, cos, sin, first_half), 0.0)
    kio_ref[:, 0:LANES] = ki.astype(BF16)
    kio_ref[:, LANES:2 * LANES] = pltpu.roll(ki, head_dim, axis=1).astype(BF16)


def _dsa_prep(proj, misc, cos_t, sin_t, tm):
    T = proj.shape[0]
    C = 512
    head_dim = C // A_HEADS
    kern = functools.partial(
        _dsa_prep_kernel, head_dim=head_dim,
        q_scale=float(head_dim ** -0.5 * np.log2(np.e)),
        idx_scale=float(head_dim ** -0.5 * IDX_HEADS ** -0.5))
    row = lambda i: (i, 0)
    return pl.pallas_call(
        kern,
        grid=(T // tm,),
        in_specs=[pl.BlockSpec((tm, C), lambda i: (i, 6)),
                  pl.BlockSpec((tm, C), lambda i: (i, 7)),
                  pl.BlockSpec((tm, C), lambda i: (i, 9)),
                  pl.BlockSpec((tm, LANES), row),
                  pl.BlockSpec((tm, LANES), row),
                  pl.BlockSpec((tm, LANES), row)],
        out_specs=[pl.BlockSpec((tm, C), row), pl.BlockSpec((tm, C), row),
                   pl.BlockSpec((tm, C), row), pl.BlockSpec((tm, 2 * LANES), row)],
        out_shape=[jax.ShapeDtypeStruct((T, C), BF16), jax.ShapeDtypeStruct((T, C), BF16),
                   jax.ShapeDtypeStruct((T, C), BF16), jax.ShapeDtypeStruct((T, 2 * LANES), BF16)],
        compiler_params=_cparams(("parallel",)),
        name="dsa_prep",
    )(proj, proj, proj, misc, cos_t, sin_t)


def _dsa_kernel(qi_ref, q_ref, misc_ref, ki_ref, k_ref, v_ref, out_ref,
                key_ref, qm_ref, sg_ref, acc_ref, m_ref, l_ref, *, top_k, kb, head_dim):
    Qb = q_ref.shape[0]
    heads = q_ref.shape[1] // head_dim
    qblk = pl.program_id(1)
    q0 = qblk * Qb
    nkb = (q0 + Qb + kb - 1) // kb
    rep = kb // LANES

    lane = lax.broadcasted_iota(I32, (Qb, LANES), 1)
    low = lane < head_dim
    misc = misc_ref[...]
    for h in range(heads):
        sl = slice((h // 2) * LANES, (h // 2 + 1) * LANES)
        mine = low if h % 2 == 0 else jnp.logical_not(low)
        qm_ref[h] = jnp.where(mine, qi_ref[:, sl], jnp.zeros((), BF16))
        qm_ref[heads + h] = jnp.where(mine, q_ref[:, sl], jnp.zeros((), BF16))
        w_h = misc[:, MISC_W + h:MISC_W + h + 1]
        sg_ref[h] = jnp.broadcast_to(jnp.where(w_h > 0, 1.0, jnp.where(w_h < 0, -1.0, 0.0)), (Qb, LANES))

    qpos = q0 + lax.broadcasted_iota(I32, (Qb, kb), 0)
    kcol = lax.broadcasted_iota(I32, (Qb, kb), 1)

    def score_block(j, carry):
        ki = ki_ref[pl.ds(pl.multiple_of(j * kb, kb), kb), :]
        score = jnp.zeros((Qb, kb), F32)
        for h in range(heads):
            kih = ki[:, (h % 2) * LANES:(h % 2 + 1) * LANES]
            logit = lax.dot_general(qm_ref[h], kih, (((1,), (1,)), ((), ())),
                                    preferred_element_type=F32)
            score = score + jnp.maximum(logit, 0.0) * _tile_lanes(sg_ref[h], rep)
        bits = pltpu.bitcast(score, I32)
        bits = jnp.where(bits == INT_MIN, 0, bits)
        skey = bits ^ ((bits >> 31) & 0x7FFFFFFF)
        key_ref[j] = jnp.where(kcol + j * kb <= qpos, skey, INT_MIN)
        return carry

    lax.fori_loop(0, nkb, score_block, 0)

    def count(pred):
        def body(j, cnt):
            blk = key_ref[j]
            for c in range(rep):
                cnt = cnt + jnp.where(pred(blk[:, c * LANES:(c + 1) * LANES], j * kb + c * LANES), 1, 0)
            return cnt
        cnt = lax.fori_loop(0, nkb, body, jnp.zeros((Qb, LANES), I32))
        return jnp.sum(cnt, axis=1, keepdims=True)

    def count_ge(thr):
        thr_b = jnp.broadcast_to(thr, (Qb, LANES))
        return count(lambda blk, base: blk >= thr_b)

    def bis_cond(st):
        bit, _, _, active = st
        return jnp.logical_and(bit >= 0, active > 0)

    def bis_body(st):
        bit, tu, done, _ = st
        cand = tu | lax.shift_left(jnp.int32(1), bit)
        cnt = count_ge(cand ^ INT_MIN)
        take = jnp.logical_and(cnt >= top_k, done == 0)
        tu = jnp.where(take, cand, tu)
        done = jnp.where(cnt == top_k, 1, done)
        return bit - 1, tu, done, jnp.max(1 - done)

    _, tu, _, _ = lax.while_loop(
        bis_cond, bis_body,
        (jnp.int32(31), jnp.zeros((Qb, 1), I32), jnp.zeros((Qb, 1), I32), jnp.int32(1)))
    thr = jnp.maximum(tu ^ INT_MIN, INT_MIN + 1)

    c_ge = count_ge(thr)
    c_gt = count_ge(thr + 1)
    need = top_k - c_gt
    excess = jnp.logical_and(c_ge > top_k, thr > INT_MIN + 1)

    @pl.when(jnp.max(excess.astype(I32)) > 0)
    def _():
        thr_b = jnp.broadcast_to(thr, (Qb, LANES))

        def count_eq_before(pos):
            pos_b = jnp.broadcast_to(pos, (Qb, LANES))
            return count(lambda blk, base: jnp.logical_and(blk == thr_b, base + lane < pos_b))

        def pos_body(t, pos):
            cand = pos | lax.shift_left(jnp.int32(1), 14 - t)
            return jnp.where(count_eq_before(cand) <= need, cand, pos)

        cut = lax.fori_loop(0, 15, pos_body, jnp.zeros((Qb, 1), I32))
        cut_b = jnp.broadcast_to(jnp.where(excess, cut, jnp.int32(2 ** 30)), (Qb, LANES))

        def demote(j, carry):
            for c in range(rep):
                blk = key_ref[j, :, c * LANES:(c + 1) * LANES]
                hit = jnp.logical_and(blk == thr_b, j * kb + c * LANES + lane >= cut_b)
                key_ref[j, :, c * LANES:(c + 1) * LANES] = jnp.where(hit, blk - 1, blk)
            return carry

        lax.fori_loop(0, nkb, demote, 0)

    acc_ref[...] = jnp.zeros_like(acc_ref)
    m_ref[...] = jnp.full(m_ref.shape, NEG_BIG, F32)
    l_ref[...] = jnp.zeros_like(l_ref)
    thr_b = jnp.broadcast_to(thr, (Qb, LANES))

    def attend_block(j, carry):
        start = pl.multiple_of(j * kb, kb)
        bias = jnp.where(key_ref[j] >= _tile_lanes(thr_b, rep), 0.0, -jnp.inf)
        for h in range(heads):
            sl = slice((h // 2) * LANES, (h // 2 + 1) * LANES)
            kh = k_ref[pl.ds(start, kb), sl]
            vh = v_ref[pl.ds(start, kb), sl]
            s = lax.dot_general(qm_ref[heads + h], kh, (((1,), (1,)), ((), ())),
                                preferred_element_type=F32) + bias
            m_old = m_ref[h]
            m_new = jnp.maximum(m_old, jnp.max(s, axis=1, keepdims=True))
            p = jnp.exp2(s - _tile_lanes(m_new, rep))
            alpha = jnp.exp2(m_old - m_new)
            l_ref[h] = alpha * l_ref[h] + jnp.sum(p, axis=1, keepdims=True)
            m_ref[h] = m_new
            pv = jnp.dot(p.astype(BF16), vh, preferred_element_type=F32)
            mine = low if h % 2 == 0 else jnp.logical_not(low)
            acc_ref[:, sl] = jnp.where(mine, alpha * acc_ref[:, sl] + pv, acc_ref[:, sl])
        return carry

    lax.fori_loop(0, nkb, attend_block, 0)

    for s2 in range(heads // 2):
        sl = slice(s2 * LANES, (s2 + 1) * LANES)
        l_pair = jnp.where(low, l_ref[2 * s2], l_ref[2 * s2 + 1])
        out_ref[:, sl] = (acc_ref[:, sl] / l_pair).astype(out_ref.dtype)


def _dsa(qi_r, q_r, misc, ki_r, k_r, proj, B, S, Qb, kb):
    T = B * S
    C = q_r.shape[1]
    nq = S // Qb
    top_k = min(TOPK_MAX, S // 4)
    kern = functools.partial(_dsa_kernel, top_k=top_k, kb=kb, head_dim=C // A_HEADS)
    once = pl.Buffered(1)
    return pl.pallas_call(
        kern,
        grid=(B, nq),
        in_specs=[pl.BlockSpec((Qb, C), lambda b, i: (b * nq + i, 0)),
                  pl.BlockSpec((Qb, C), lambda b, i: (b * nq + i, 0)),
                  pl.BlockSpec((Qb, LANES), lambda b, i: (b * nq + i, 0)),
                  pl.BlockSpec((S, 2 * LANES), lambda b, i: (b, 0), pipeline_mode=once),
                  pl.BlockSpec((S, C), lambda b, i: (b, 0), pipeline_mode=once),
                  pl.BlockSpec((S, C), lambda b, i: (b, 8), pipeline_mode=once)],
        out_specs=pl.BlockSpec((Qb, C), lambda b, i: (b * nq + i, 0)),
        out_shape=jax.ShapeDtypeStruct((T, C), BF16),
        scratch_shapes=[pltpu.VMEM((S // kb, Qb, kb), I32),
                        pltpu.VMEM((2 * A_HEADS, Qb, LANES), BF16),
                        pltpu.VMEM((IDX_HEADS, Qb, LANES), F32),
                        pltpu.VMEM((Qb, C), F32),
                        pltpu.VMEM((A_HEADS, Qb, LANES), F32),
                        pltpu.VMEM((A_HEADS, Qb, LANES), F32)],
        compiler_params=_cparams(("parallel", "arbitrary")),
        name="dsa",
    )(qi_r, q_r, misc, ki_r, k_r, proj)


def _merge_kernel(ya_ref, yb_ref, yc_ref, ga_ref, gb_ref, gc_ref, x_ref, wb_ref, wo_ref, out_ref):
    def branch(y_ref, g_ref, n):
        return _sigmoid(g_ref[...].astype(F32)) * jnp.dot(y_ref[...], wb_ref[n], preferred_element_type=F32)

    merged = branch(ya_ref, ga_ref, 0) + branch(yb_ref, gb_ref, 1) + branch(yc_ref, gc_ref, 2)
    out_ref[...] = x_ref[...] + jnp.dot(merged.astype(BF16), wo_ref[...], preferred_element_type=F32)


def _merge(ya, yb, yc, proj, x, wb, wo, tm):
    T, D = x.shape
    C = ya.shape[1]
    row = lambda i: (i, 0)
    gate0 = 5120 // D
    return pl.pallas_call(
        _merge_kernel,
        grid=(T // tm,),
        in_specs=[pl.BlockSpec((tm, C), row), pl.BlockSpec((tm, C), row), pl.BlockSpec((tm, C), row),
                  pl.BlockSpec((tm, D), lambda i: (i, gate0)),
                  pl.BlockSpec((tm, D), lambda i: (i, gate0 + 1)),
                  pl.BlockSpec((tm, D), lambda i: (i, gate0 + 2)),
                  pl.BlockSpec((tm, D), row),
                  pl.BlockSpec((3, C, D), lambda i: (0, 0, 0)),
                  pl.BlockSpec((D, D), lambda i: (0, 0))],
        out_specs=pl.BlockSpec((tm, D), row),
        out_shape=jax.ShapeDtypeStruct((T, D), F32),
        compiler_params=_cparams(("parallel",)),
        name="merge_out",
    )(ya, yb, yc, proj, proj, proj, x, wb, wo)


def _ffn_down_kernel(g_ref, u_ref, cw_ref, cb_ref, wd_ref, x_ref, out_ref, xs_ref):
    first = pl.program_id(1) == 0
    g = _causal_conv(xs_ref, g_ref[...].astype(F32), cw_ref, cb_ref, first, FFN_CONV)
    a = _gelu(g) * u_ref[...].astype(F32)
    out_ref[...] = x_ref[...] + jnp.dot(a.astype(BF16), wd_ref[...], preferred_element_type=F32)


def _ffn_down(up, conv_w, conv_b, wd, x, B, S, tm):
    T, D = x.shape
    F = wd.shape[0]
    ns = S // tm
    return pl.pallas_call(
        _ffn_down_kernel,
        grid=(B, ns),
        in_specs=[pl.BlockSpec((tm, F), lambda b, s: (b * ns + s, 0)),
                  pl.BlockSpec((tm, F), lambda b, s: (b * ns + s, 1)),
                  pl.BlockSpec((FFN_CONV, F), lambda b, s: (0, 0)),
                  pl.BlockSpec((1, F), lambda b, s: (0, 0)),
                  pl.BlockSpec((F, D), lambda b, s: (0, 0)),
                  pl.BlockSpec((tm, D), lambda b, s: (b * ns + s, 0))],
        out_specs=pl.BlockSpec((tm, D), lambda b, s: (b * ns + s, 0)),
        out_shape=jax.ShapeDtypeStruct((T, D), F32),
        scratch_shapes=[pltpu.VMEM((tm + SUBLANES, F), F32)],
        compiler_params=_cparams(("parallel", "arbitrary")),
        name="ffn_down",
    )(up, up, conv_w, conv_b, wd, x)


def _final_norm_kernel(x_ref, g_ref, o_ref):
    x = x_ref[...]
    ms = jnp.mean(x * x, axis=-1, keepdims=True)
    o_ref[...] = x * lax.rsqrt(ms + EPS) * g_ref[...]


def _final_norm(x, g, tm):
    T, D = x.shape
    return pl.pallas_call(
        _final_norm_kernel,
        grid=(T // tm,),
        in_specs=[pl.BlockSpec((tm, D), lambda i: (i, 0)), pl.BlockSpec((1, D), lambda i: (0, 0))],
        out_specs=pl.BlockSpec((tm, D), lambda i: (i, 0)),
        out_shape=jax.ShapeDtypeStruct((T, D), F32),
        compiler_params=_cparams(("parallel",)),
        name="final_norm",
    )(x, g)


def _block_diag(w):
    n, d, e = w.shape
    eye = jnp.eye(n, dtype=w.dtype)
    return (w[:, :, None, :] * eye[:, None, :, None]).reshape(n * d, n * e)


def _split_in_proj(w_in, d_branch):
    sizes = (2 * d_branch, d_branch, d_branch, M_HEADS, M_HEADS, d_branch, d_branch, d_branch, d_branch,
             d_branch, d_branch, d_branch // IDX_HEADS, IDX_HEADS, 3 * w_in.shape[0])
    offs = np.cumsum((0,) + sizes)
    assert offs[-1] == w_in.shape[1]
    (m_qk, m_v, m_o, m_i, m_f, r_x, r_g, a_q, a_k, a_v, a_qi, a_ki, a_w, gates) = [
        w_in[:, offs[n]:offs[n + 1]] for n in range(len(sizes))]
    main = jnp.concatenate([m_qk, m_v, m_o, r_x, r_g, a_q, a_k, a_v, a_qi, gates], axis=1).astype(BF16)
    pad = jnp.zeros((w_in.shape[0], LANES - MISC_W - IDX_HEADS), w_in.dtype)
    side = jnp.concatenate([a_ki, m_i, m_f, a_w, pad], axis=1).astype(BF16)
    return main, side


def kernel(x, positions, norm_mix, w_in, mlstm_conv_w, mlstm_conv_b, mlstm_i_bias, mlstm_f_bias, mlstm_norm, rglru_conv_w, rglru_conv_b, rglru_w_a, rglru_b_a, rglru_w_x, rglru_b_x, rglru_lambda, w_branch, w_out, norm_ffn, ffn_up, ffn_conv_w, ffn_conv_b, ffn_down, norm_final):
    B, S, D = x.shape
    T = B * S
    depth = w_in.shape[0]
    d_branch = mlstm_norm.shape[1]
    d_ff = ffn_down.shape[1]
    assert D == 1024 and d_branch == 512 and d_ff == 3 * D, "column-block indexing assumes these widths"

    tm = min(1024, S)
    chunk = min(128, S)
    q_block = min(128, S)
    k_block = min(512, S)

    cos_t, sin_t = _rope_tables(positions, d_branch // A_HEADS)
    xf = x.reshape(T, D)
    for l in range(depth):
        w_main, w_side = _split_in_proj(w_in[l], d_branch)
        g_mix = norm_mix[l][None, :]
        proj = _norm_matmul(xf, g_mix, w_main, BF16, tm, 512, "in_proj")
        misc = _norm_matmul(xf, g_mix, w_side, F32, tm, LANES, "in_proj_side")

        gate_bias = jnp.zeros((1, LANES), F32)
        gate_bias = gate_bias.at[0, MISC_I:MISC_I + M_HEADS].set(mlstm_i_bias[l])
        gate_bias = gate_bias.at[0, MISC_F:MISC_F + M_HEADS].set(mlstm_f_bias[l])
        y_a = _mlstm(proj, misc, mlstm_conv_w[l], mlstm_conv_b[l][None, :], gate_bias,
                     mlstm_norm[l][None, :], B, S, chunk)

        y_b = _rglru(proj, rglru_conv_w[l], rglru_conv_b[l][None, :],
                     _block_diag(rglru_w_a[l]).astype(BF16), rglru_b_a[l][None, :],
                     _block_diag(rglru_w_x[l]).astype(BF16), rglru_b_x[l][None, :],
                     rglru_lambda[l][None, :], B, S, min(256, S))

        q_r, k_r, qi_r, ki_r = _dsa_prep(proj, misc, cos_t, sin_t, tm)
        y_c = _dsa(qi_r, q_r, misc, ki_r, k_r, proj, B, S, q_block, k_block)

        xf = _merge(y_a, y_b, y_c, proj, xf, w_branch[l].astype(BF16), w_out[l].astype(BF16), min(512, S))

        up = _norm_matmul(xf, norm_ffn[l][None, :], ffn_up[l].astype(BF16), BF16, tm, 512, "ffn_up")
        xf = _ffn_down(up, ffn_conv_w[l], ffn_conv_b[l][None, :], ffn_down[l].astype(BF16), xf,
                       B, S, min(256, S))
    return _final_norm(xf, norm_final[None, :], tm).reshape(B, S, D)
```

```python
import functools

import jax
import jax.numpy as jnp
import numpy as np
from jax import lax
from jax.experimental import pallas as pl
from jax.experimental.pallas import tpu as pltpu

F32 = jnp.float32
BF16 = jnp.bfloat16
I32 = jnp.int32

EPS = 1e-6
ROPE_THETA = 10000.0
R_C = 8.0
TOPK_MAX = 256
INT_MIN = -(2 ** 31)
NEG_BIG = -1e30

LANES = 128
SUBLANES = 8
VMEM_LIMIT = 56 * 1024 * 1024
MOMENT_STRIDE = 4

M_HEADS = 4
R_BLOCKS = 8
A_HEADS = 8
IDX_HEADS = 8
M_CONV = 4
R_CONV = 4
FFN_CONV = 3

MISC_KI = 0
MISC_I = 64
MISC_F = 68
MISC_W = 72


def _cparams(sem):
    return pltpu.CompilerParams(dimension_semantics=sem, vmem_limit_bytes=VMEM_LIMIT)


def _sigmoid(x):
    return 1.0 / (1.0 + jnp.exp(-x))


def _log_sigmoid(x):
    return jnp.minimum(x, 0.0) - jnp.log1p(jnp.exp(-jnp.abs(x)))


def _expm1(y):
    p = jnp.ones_like(y)
    for n in range(13, 1, -1):
        p = 1.0 + (y * (1.0 / n)) * p
    return jnp.where(jnp.abs(y) < 0.25, y * p, jnp.exp(y) - 1.0)


def _gelu(x):
    return 0.5 * x * (1.0 + jnp.tanh(0.7978845608028654 * (x + 0.044715 * (x * x * x))))


def _tile_lanes(x, n):
    return jnp.concatenate([x] * n, axis=1)


def _causal_conv(xs_ref, x, w_ref, b_ref, first, taps):
    tm = x.shape[0]

    @pl.when(first)
    def _():
        xs_ref[0:SUBLANES, :] = jnp.zeros((SUBLANES, x.shape[1]), F32)

    xs_ref[SUBLANES:SUBLANES + tm, :] = x
    y = b_ref[...] + w_ref[taps - 1:taps, :] * x
    for j in range(taps - 1):
        d = taps - 1 - j
        y = y + w_ref[j:j + 1, :] * xs_ref[SUBLANES - d:SUBLANES - d + tm, :]
    xs_ref[0:SUBLANES, :] = xs_ref[tm:tm + SUBLANES, :]
    return y


def _rope_table_kernel(pos_ref, inv_ref, sgn_ref, cos_ref, sin_ref):
    ang = pos_ref[...] * inv_ref[...]
    cos_ref[...] = jnp.cos(ang)
    sin_ref[...] = jnp.sin(ang) * sgn_ref[...]


def _rope_tables(positions, head_dim):
    T = positions.size
    half = head_dim // 2
    inv = ROPE_THETA ** (-jnp.arange(0, head_dim, 2, dtype=F32) / head_dim)
    lane = np.arange(LANES)
    inv_row = inv[lane % half][None, :]
    sgn_row = jnp.asarray(np.where((lane % head_dim) < half, -1.0, 1.0), F32)[None, :]
    pos = positions.reshape(T, 1).astype(F32)
    tm = min(T, 1024)
    return pl.pallas_call(
        _rope_table_kernel,
        grid=(T // tm,),
        in_specs=[pl.BlockSpec((tm, 1), lambda i: (i, 0)),
                  pl.BlockSpec((1, LANES), lambda i: (0, 0)),
                  pl.BlockSpec((1, LANES), lambda i: (0, 0))],
        out_specs=[pl.BlockSpec((tm, LANES), lambda i: (i, 0)),
                   pl.BlockSpec((tm, LANES), lambda i: (i, 0))],
        out_shape=[jax.ShapeDtypeStruct((T, LANES), F32)] * 2,
        compiler_params=_cparams(("parallel",)),
        name="rope_tables",
    )(pos, inv_row, sgn_row)


def _norm_matmul_kernel(x_ref, g_ref, w_ref, o_ref, h_ref):
    @pl.when(pl.program_id(1) == 0)
    def _():
        x = x_ref[...]
        ms = jnp.mean(x * x, axis=-1, keepdims=True)
        h_ref[...] = (x * lax.rsqrt(ms + EPS) * g_ref[...]).astype(BF16)

    o_ref[...] = jnp.dot(h_ref[...], w_ref[...], preferred_element_type=F32).astype(o_ref.dtype)


def _norm_matmul(x, g, w, out_dtype, tm, tn, name):
    T, D = x.shape
    N = w.shape[1]
    return pl.pallas_call(
        _norm_matmul_kernel,
        grid=(T // tm, N // tn),
        in_specs=[pl.BlockSpec((tm, D), lambda i, j: (i, 0)),
                  pl.BlockSpec((1, D), lambda i, j: (0, 0)),
                  pl.BlockSpec((D, tn), lambda i, j: (0, j))],
        out_specs=pl.BlockSpec((tm, tn), lambda i, j: (i, j)),
        out_shape=jax.ShapeDtypeStruct((T, N), out_dtype),
        scratch_shapes=[pltpu.VMEM((tm, D), BF16)],
        compiler_params=_cparams(("parallel", "arbitrary")),
        name=name,
    )(x, g, w)


def _row_cumsum(x):
    n = x.shape[0]
    row = lax.broadcasted_iota(I32, x.shape, 0)
    d = 1
    while d < n:
        x = x + jnp.where(row >= d, pltpu.roll(x, d, axis=0), 0.0)
        d *= 2
    return x


def _mlstm_kernel(qk_ref, v_ref, o_ref, misc_ref, cw_ref, cb_ref, gb_ref, gn_ref, out_ref,
                  xs_ref, ct_ref, n_ref, m_ref, *, heads, dh):
    first = pl.program_id(1) == 0

    @pl.when(first)
    def _():
        ct_ref[...] = jnp.zeros_like(ct_ref)
        n_ref[...] = jnp.zeros_like(n_ref)
        m_ref[...] = jnp.zeros_like(m_ref)

    L = qk_ref.shape[0]
    hd = heads * dh
    qk = _causal_conv(xs_ref, qk_ref[...].astype(F32), cw_ref, cb_ref, first, M_CONV)
    qk = qk * _sigmoid(qk)

    pre = misc_ref[...] + gb_ref[...]
    g_all = _row_cumsum(_log_sigmoid(pre))
    g_t = g_all.T
    pre_t = pre.T
    row = lax.broadcasted_iota(I32, (L, L), 0)
    col = lax.broadcasted_iota(I32, (L, L), 1)
    causal = row >= col

    for h in range(heads):
        q = qk[:, h * dh:(h + 1) * dh]
        k = qk[:, hd + h * dh:hd + (h + 1) * dh] * (dh ** -0.5)
        v = v_ref[:, h * dh:(h + 1) * dh]
        g_c = g_all[:, MISC_F + h:MISC_F + h + 1]
        i_c = pre[:, MISC_I + h:MISC_I + h + 1]
        g_r = g_t[MISC_F + h:MISC_F + h + 1, :]
        i_r = pre_t[MISC_I + h:MISC_I + h + 1, :]
        m_prev = m_ref[h:h + 1, 0:1]

        logd = jnp.where(causal, g_c - g_r + i_r, -jnp.inf)
        log_inter = g_c + m_prev
        m_t = jnp.maximum(log_inter, jnp.max(logd, axis=-1, keepdims=True))
        qb = q.astype(BF16)
        kb = k.astype(BF16)
        s = lax.dot_general(qb, kb, (((1,), (1,)), ((), ())), preferred_element_type=F32)
        s = s * jnp.exp(logd - m_t)
        w_inter = jnp.exp(log_inter - m_t)
        num = jnp.dot(s.astype(BF16), v, preferred_element_type=F32)
        num = num + w_inter * jnp.dot(qb, ct_ref[h].astype(BF16), preferred_element_type=F32)
        den = jnp.sum(s, axis=-1, keepdims=True)
        den = den + w_inter * jnp.sum(q * n_ref[h:h + 1, :], axis=-1, keepdims=True)
        hh = num / jnp.maximum(jnp.abs(den), jnp.exp(-m_t))

        g_last = g_c[L - 1:L, :]
        m_new = m_t[L - 1:L, :]
        w_state = jnp.exp(g_last - g_c + i_c - m_new)
        decay = jnp.exp(g_last + m_prev - m_new)
        kw = k * w_state
        ct_ref[h] = decay * ct_ref[h] + lax.dot_general(
            kw.astype(BF16), v, (((0,), (0,)), ((), ())), preferred_element_type=F32)
        n_ref[h:h + 1, :] = decay * n_ref[h:h + 1, :] + jnp.sum(kw, axis=0, keepdims=True)
        m_ref[h:h + 1, :] = jnp.broadcast_to(m_new, (1, LANES))

        mu = jnp.mean(hh, axis=-1, keepdims=True)
        hc = hh - mu
        var = jnp.mean(hc * hc, axis=-1, keepdims=True)
        y = hc * lax.rsqrt(var + EPS) * gn_ref[:, h * dh:(h + 1) * dh]
        y = y * _sigmoid(o_ref[:, h * dh:(h + 1) * dh].astype(F32))
        out_ref[:, h * dh:(h + 1) * dh] = y.astype(out_ref.dtype)


def _mlstm(proj, misc, conv_w, conv_b, gate_bias, norm_g, B, S, L):
    T = B * S
    nc = S // L
    hd = norm_g.shape[1]
    dh = hd // M_HEADS
    kern = functools.partial(_mlstm_kernel, heads=M_HEADS, dh=dh)
    return pl.pallas_call(
        kern,
        grid=(B, nc),
        in_specs=[pl.BlockSpec((L, 2 * hd), lambda b, c: (b * nc + c, 0)),
                  pl.BlockSpec((L, hd), lambda b, c: (b * nc + c, 2)),
                  pl.BlockSpec((L, hd), lambda b, c: (b * nc + c, 3)),
                  pl.BlockSpec((L, LANES), lambda b, c: (b * nc + c, 0)),
                  pl.BlockSpec((M_CONV, 2 * hd), lambda b, c: (0, 0)),
                  pl.BlockSpec((1, 2 * hd), lambda b, c: (0, 0)),
                  pl.BlockSpec((1, LANES), lambda b, c: (0, 0)),
                  pl.BlockSpec((1, hd), lambda b, c: (0, 0))],
        out_specs=pl.BlockSpec((L, hd), lambda b, c: (b * nc + c, 0)),
        out_shape=jax.ShapeDtypeStruct((T, hd), BF16),
        scratch_shapes=[pltpu.VMEM((L + SUBLANES, 2 * hd), F32),
                        pltpu.VMEM((M_HEADS, dh, dh), F32),
                        pltpu.VMEM((SUBLANES, dh), F32),
                        pltpu.VMEM((SUBLANES, LANES), F32)],
        compiler_params=_cparams(("parallel", "arbitrary")),
        name="mlstm",
    )(proj, proj, proj, misc, conv_w, conv_b, gate_bias, norm_g)


def _rglru_kernel(x_ref, g_ref, cw_ref, cb_ref, wa_ref, ba_ref, wx_ref, bx_ref, lam_ref, out_ref,
                  xs_ref, h_ref):
    first = pl.program_id(1) == 0

    @pl.when(first)
    def _():
        h_ref[...] = jnp.zeros_like(h_ref)

    tm = x_ref.shape[0]
    x = _causal_conv(xs_ref, x_ref[...].astype(F32), cw_ref, cb_ref, first, R_CONV)
    xb = x.astype(BF16)
    r = _sigmoid(jnp.dot(xb, wa_ref[...], preferred_element_type=F32) + ba_ref[...])
    i = _sigmoid(jnp.dot(xb, wx_ref[...], preferred_element_type=F32) + bx_ref[...])
    lam = lam_ref[...]
    softplus_neg_lam = jnp.maximum(-lam, 0.0) + jnp.log1p(jnp.exp(-jnp.abs(lam)))
    log_a = -R_C * r * softplus_neg_lam
    a = jnp.exp(log_a)
    u = jnp.sqrt(-_expm1(2.0 * log_a)) * (i * x)

    row = lax.broadcasted_iota(I32, a.shape, 0)
    d = 1
    while d < tm:
        valid = row >= d
        u = jnp.where(valid, a * pltpu.roll(u, d, axis=0) + u, u)
        a = jnp.where(valid, a * pltpu.roll(a, d, axis=0), a)
        d *= 2
    hcur = a * h_ref[0:1, :] + u
    h_ref[0:1, :] = hcur[tm - 1:tm, :]
    out_ref[...] = (_gelu(g_ref[...].astype(F32)) * hcur).astype(out_ref.dtype)


def _rglru(proj, conv_w, conv_b, wa, ba, wx, bx, lam, B, S, tm):
    T = B * S
    ns = S // tm
    C = lam.shape[1]
    vec = pl.BlockSpec((1, C), lambda b, s: (0, 0))
    mat = pl.BlockSpec((C, C), lambda b, s: (0, 0))
    return pl.pallas_call(
        _rglru_kernel,
        grid=(B, ns),
        in_specs=[pl.BlockSpec((tm, C), lambda b, s: (b * ns + s, 4)),
                  pl.BlockSpec((tm, C), lambda b, s: (b * ns + s, 5)),
                  pl.BlockSpec((R_CONV, C), lambda b, s: (0, 0)), vec,
                  mat, vec, mat, vec, vec],
        out_specs=pl.BlockSpec((tm, C), lambda b, s: (b * ns + s, 0)),
        out_shape=jax.ShapeDtypeStruct((T, C), BF16),
        scratch_shapes=[pltpu.VMEM((tm + SUBLANES, C), F32),
                        pltpu.VMEM((SUBLANES, C), F32)],
        compiler_params=_cparams(("parallel", "arbitrary")),
        name="rglru",
    )(proj, proj, conv_w, conv_b, wa, ba, wx, bx, lam)


def _rope_slab(x, cos, sin_signed, first_half):
    fwd = pltpu.roll(x, 32, axis=1)
    bwd = pltpu.roll(x, 96, axis=1)
    return x * cos + jnp.where(first_half, bwd, fwd) * sin_signed


def _dsa_prep_kernel(q_ref, k_ref, qi_ref, v_ref, misc_ref, cos_ref, sin_ref,
                     qo_ref, ko_ref, qio_ref, kio_ref, vt_ref, *, q_scale, idx_scale, head_dim):
    kb = vt_ref.shape[2]
    for c in range(vt_ref.shape[0]):
        vt_ref[c] = v_ref[c * kb:(c + 1) * kb, :].astype(F32).T.astype(BF16)
    tm = q_ref.shape[0]
    cos = cos_ref[...]
    sin = sin_ref[...]
    lane = lax.broadcasted_iota(I32, (tm, LANES), 1)
    first_half = (lane % head_dim) < (head_dim // 2)
    low = lane < head_dim
    misc = misc_ref[...]
    nslab = q_ref.shape[1] // LANES
    for s in range(nslab):
        sl = slice(s * LANES, (s + 1) * LANES)
        qo_ref[:, sl] = (_rope_slab(q_ref[:, sl].astype(F32), cos, sin, first_half) * q_scale).astype(BF16)
        ko_ref[:, sl] = _rope_slab(k_ref[:, sl].astype(F32), cos, sin, first_half).astype(BF16)
        w_even = jnp.abs(misc[:, MISC_W + 2 * s:MISC_W + 2 * s + 1])
        w_odd = jnp.abs(misc[:, MISC_W + 2 * s + 1:MISC_W + 2 * s + 2])
        w = jnp.where(low, w_even, w_odd) * idx_scale
        qio_ref[:, sl] = (_rope_slab(qi_ref[:, sl].astype(F32), cos, sin, first_half) * w).astype(BF16)
    ki = jnp.where(low, _rope_slab(misc, cos, sin, first_half), 0.0)
    kio_ref[:, 0:LANES] = ki.astype(BF16)
    kio_ref[:, LANES:2 * LANES] = pltpu.roll(ki, head_dim, axis=1).astype(BF16)


def _dsa_prep(proj, misc, cos_t, sin_t, tm, kb):
    T = proj.shape[0]
    C = 512
    head_dim = C // A_HEADS
    kern = functools.partial(
        _dsa_prep_kernel, head_dim=head_dim,
        q_scale=float(head_dim ** -0.5 * np.log2(np.e)),
        idx_scale=float(head_dim ** -0.5 * IDX_HEADS ** -0.5))
    row = lambda i: (i, 0)
    return pl.pallas_call(
        kern,
        grid=(T // tm,),
        in_specs=[pl.BlockSpec((tm, C), lambda i: (i, 6)),
                  pl.BlockSpec((tm, C), lambda i: (i, 7)),
                  pl.BlockSpec((tm, C), lambda i: (i, 9)),
                  pl.BlockSpec((tm, C), lambda i: (i, 8)),
                  pl.BlockSpec((tm, LANES), row),
                  pl.BlockSpec((tm, LANES), row),
                  pl.BlockSpec((tm, LANES), row)],
        out_specs=[pl.BlockSpec((tm, C), row), pl.BlockSpec((tm, C), row),
                   pl.BlockSpec((tm, C), row), pl.BlockSpec((tm, 2 * LANES), row),
                   pl.BlockSpec((tm // kb, C, kb), lambda i: (i, 0, 0))],
        out_shape=[jax.ShapeDtypeStruct((T, C), BF16), jax.ShapeDtypeStruct((T, C), BF16),
                   jax.ShapeDtypeStruct((T, C), BF16), jax.ShapeDtypeStruct((T, 2 * LANES), BF16),
                   jax.ShapeDtypeStruct((T // kb, C, kb), BF16)],
        compiler_params=_cparams(("parallel",)),
        name="dsa_prep",
    )(proj, proj, proj, proj, misc, cos_t, sin_t)


def _dsa_kernel(qi_ref, q_ref, misc_ref, ki_ref, k_ref, vt_ref, out_ref,
                key_ref, qm_ref, s_ref, acc_ref, m_ref, l_ref, *, top_k, kb, head_dim):
    Qb = q_ref.shape[0]
    heads = q_ref.shape[1] // head_dim
    qblk = pl.program_id(1)
    q0 = qblk * Qb
    nkb = (q0 + Qb + kb - 1) // kb
    ngrp = kb // SUBLANES
    nt = (((1,), (1,)), ((), ()))

    lane = lax.broadcasted_iota(I32, (Qb, LANES), 1)
    low = lane < head_dim
    for h in range(heads):
        sl = slice((h // 2) * LANES, (h // 2 + 1) * LANES)
        mine = low if h % 2 == 0 else jnp.logical_not(low)
        qm_ref[h] = jnp.where(mine, qi_ref[:, sl], jnp.zeros((), BF16))
        qm_ref[heads + h] = jnp.where(mine, q_ref[:, sl], jnp.zeros((), BF16))
    w_t = misc_ref[...].T
    sgn_t = jnp.where(w_t > 0, 1.0, jnp.where(w_t < 0, -1.0, 0.0))

    qpos = q0 + lax.broadcasted_iota(I32, (kb, Qb), 1)
    krow = lax.broadcasted_iota(I32, (kb, Qb), 0)
    sub = lax.broadcasted_iota(I32, (SUBLANES, Qb), 0)

    def score_block(j, carry):
        ki = ki_ref[pl.ds(pl.multiple_of(j * kb, kb), kb), :]
        for hp in range(heads // 2):
            part = None
            for h in (2 * hp, 2 * hp + 1):
                kih = ki[:, (h % 2) * LANES:(h % 2 + 1) * LANES]
                logit = lax.dot_general(kih, qm_ref[h], nt, preferred_element_type=F32)
                term = jnp.maximum(logit, 0.0) * sgn_t[MISC_W + h:MISC_W + h + 1, :]
                part = term if part is None else part + term
            if hp == 0:
                s_ref[0] = part
            else:
                s_ref[0] += part
        bits = pltpu.bitcast(s_ref[0], I32)
        bits = jnp.where(bits == INT_MIN, 0, bits)
        skey = bits ^ ((bits >> 31) & 0x7FFFFFFF)
        key_ref[j] = jnp.where(krow + j * kb <= qpos, skey, INT_MIN)
        n_s, sum_s, sq_s = carry
        for r in range(0, ngrp, MOMENT_STRIDE):
            rows = slice(r * SUBLANES, (r + 1) * SUBLANES)
            vis = (sub + (j * kb + r * SUBLANES)) <= qpos[0:SUBLANES, :]
            sv = jnp.where(vis, s_ref[0, rows, :], 0.0)
            n_s = n_s + jnp.where(vis, 1.0, 0.0)
            sum_s = sum_s + sv
            sq_s = sq_s + sv * sv
        return n_s, sum_s, sq_s

    zero8 = jnp.zeros((SUBLANES, Qb), F32)
    n_s, sum_s, sq_s = lax.fori_loop(0, nkb, score_block, (zero8, zero8, zero8))
    n_s = jnp.maximum(jnp.sum(n_s, axis=0, keepdims=True), 1.0)
    mean = jnp.sum(sum_s, axis=0, keepdims=True) / n_s
    std = jnp.sqrt(jnp.maximum(jnp.sum(sq_s, axis=0, keepdims=True) / n_s - mean * mean, 0.0))

    def count(pred):
        def body(j, cnt):
            for r in range(ngrp):
                grp = key_ref[j, r * SUBLANES:(r + 1) * SUBLANES, :]
                cnt = cnt + jnp.where(pred(grp, j * kb + r * SUBLANES), 1, 0)
            return cnt
        cnt = lax.fori_loop(0, nkb, body, jnp.zeros((SUBLANES, Qb), I32))
        return jnp.sum(cnt, axis=0, keepdims=True)

    def count_ge(thr):
        thr_b = jnp.broadcast_to(thr, (SUBLANES, Qb))
        return count(lambda grp, base: grp >= thr_b)

    n_vis = q0 + 1 + lax.broadcasted_iota(I32, (1, Qb), 1)
    nf = n_vis.astype(F32)

    def quantile(c):
        p = jnp.clip(c, 0.5, nf - 0.5) / nf
        t = jnp.sqrt(-2.0 * jnp.log(jnp.minimum(p, 1.0 - p)))
        z = t - (2.515517 + t * (0.802853 + t * 0.010328)) / (
            1.0 + t * (1.432788 + t * (0.189269 + t * 0.001308)))
        return jnp.where(p < 0.5, z, -z)

    def key_flip(k):
        return k ^ ((k >> 31) & 0x7FFFFFFF)

    z_goal = quantile(jnp.float32(top_k - 0.5))
    trivial = n_vis <= top_k

    def search_cond(st):
        return jnp.logical_and(st[0] < 100, st[-1] > 0)

    def search_body(st):
        it, lo, hi, c_lo, c_hi, va, za, vb, zb, thr, c_thr, done, _ = st
        guess = va + (z_goal - za) / (zb - za) * (vb - va)
        guess_k = key_flip(pltpu.bitcast(guess, I32))
        mid = (lo >> 1) + (hi >> 1) + (lo & hi & 1)
        use_guess = jnp.logical_and(it % 3 != 2, jnp.abs(guess) < 3e38)
        cand = jnp.where(use_guess, jnp.clip(guess_k, lo + 1, hi - 1), mid)
        cand = jnp.where(done > 0, thr, cand)
        cnt = count_ge(cand)
        act = done == 0
        up = jnp.logical_and(act, cnt >= top_k)
        dn = jnp.logical_and(act, cnt < top_k)
        lo = jnp.where(up, cand, lo)
        c_lo = jnp.where(up, cnt, c_lo)
        hi = jnp.where(dn, cand, hi)
        c_hi = jnp.where(dn, cnt, c_hi)
        cv = pltpu.bitcast(key_flip(cand), F32)
        zc = quantile(cnt.astype(F32))
        va = jnp.where(up, cv, va)
        za = jnp.where(up, zc, za)
        vb = jnp.where(dn, cv, vb)
        zb = jnp.where(dn, zc, zb)
        hit = jnp.logical_and(act, cnt == top_k)
        shut = jnp.logical_and(act, hi - 1 <= lo)
        thr = jnp.where(hit, cand, jnp.where(shut, lo, thr))
        c_thr = jnp.where(hit, cnt, jnp.where(shut, c_lo, c_thr))
        done = jnp.where(jnp.logical_or(hit, shut), 1, done)
        return it + 1, lo, hi, c_lo, c_hi, va, za, vb, zb, thr, c_thr, done, jnp.max(1 - done)

    row_i = lambda v: jnp.full((1, Qb), v, I32)
    row_f = lambda v: jnp.full((1, Qb), v, F32)
    st = lax.while_loop(search_cond, search_body, (
        jnp.int32(0), row_i(INT_MIN + 1), row_i(2 ** 31 - 1), n_vis, row_i(0),
        mean - 3.0 * std, row_f(-3.0), mean + 5.0 * std, row_f(5.0),
        row_i(INT_MIN + 1), jnp.where(trivial, n_vis, 0), trivial.astype(I32),
        jnp.max(1 - trivial.astype(I32))))
    thr, c_thr, c_gt = st[9], st[10], st[4]

    need = top_k - c_gt
    excess = c_thr > top_k
    thr_b = jnp.broadcast_to(thr, (SUBLANES, Qb))

    @pl.when(jnp.max(excess.astype(I32)) > 0)
    def _():
        def count_eq_before(pos):
            pos_b = jnp.broadcast_to(pos, (SUBLANES, Qb))
            return count(lambda grp, base: jnp.logical_and(grp == thr_b, base + sub < pos_b))

        def pos_body(t, pos):
            cand = pos | lax.shift_left(jnp.int32(1), 14 - t)
            return jnp.where(count_eq_before(cand) <= need, cand, pos)

        cut = lax.fori_loop(0, 15, pos_body, jnp.zeros((1, Qb), I32))
        cut_b = jnp.broadcast_to(jnp.where(excess, cut, jnp.int32(2 ** 30)), (SUBLANES, Qb))

        def demote(j, carry):
            for r in range(ngrp):
                rows = slice(r * SUBLANES, (r + 1) * SUBLANES)
                grp = key_ref[j, rows, :]
                hit = jnp.logical_and(grp == thr_b, j * kb + r * SUBLANES + sub >= cut_b)
                key_ref[j, rows, :] = jnp.where(hit, grp - 1, grp)
            return carry

        lax.fori_loop(0, nkb, demote, 0)

    acc_ref[...] = jnp.zeros_like(acc_ref)
    m_ref[...] = jnp.full(m_ref.shape, NEG_BIG, F32)
    l_ref[...] = jnp.zeros_like(l_ref)
    thr_full = jnp.broadcast_to(thr, (kb, Qb))

    def attend_block(j, carry):
        start = pl.multiple_of(j * kb, kb)
        bias = jnp.where(key_ref[j] >= thr_full, 0.0, -jnp.inf)
        m_blk = []
        for h in range(heads):
            sl = slice((h // 2) * LANES, (h // 2 + 1) * LANES)
            s = lax.dot_general(k_ref[pl.ds(start, kb), sl], qm_ref[heads + h], nt,
                                preferred_element_type=F32) + bias
            s_ref[h] = s
            m_blk.append(jnp.max(s, axis=0, keepdims=True))
        for h in range(heads):
            dims = slice(h * head_dim, (h + 1) * head_dim)
            m_old = m_ref[h]
            m_new = jnp.maximum(m_old, m_blk[h])
            p = jnp.exp2(s_ref[h] - m_new)
            alpha = jnp.exp2(m_old - m_new)
            l_ref[h] = alpha * l_ref[h] + jnp.sum(p, axis=0, keepdims=True)
            m_ref[h] = m_new
            pv = jnp.dot(vt_ref[j, dims, :], p.astype(BF16),
                         preferred_element_type=F32)
            acc_ref[dims, :] = alpha * acc_ref[dims, :] + pv
        return carry

    lax.fori_loop(0, nkb, attend_block, 0)

    for h in range(heads):
        dims = slice(h * head_dim, (h + 1) * head_dim)
        acc_ref[dims, :] = acc_ref[dims, :] / l_ref[h]
    out_ref[...] = acc_ref[...].T.astype(out_ref.dtype)


def _dsa(qi_r, q_r, misc, ki_r, k_r, v_t, B, S, Qb, kb):
    T = B * S
    C = q_r.shape[1]
    nq = S // Qb
    top_k = min(TOPK_MAX, S // 4)
    kern = functools.partial(_dsa_kernel, top_k=top_k, kb=kb, head_dim=C // A_HEADS)
    once = pl.Buffered(1)
    return pl.pallas_call(
        kern,
        grid=(B, nq),
        in_specs=[pl.BlockSpec((Qb, C), lambda b, i: (b * nq + i, 0)),
                  pl.BlockSpec((Qb, C), lambda b, i: (b * nq + i, 0)),
                  pl.BlockSpec((Qb, LANES), lambda b, i: (b * nq + i, 0)),
                  pl.BlockSpec((S, 2 * LANES), lambda b, i: (b, 0), pipeline_mode=once),
                  pl.BlockSpec((S, C), lambda b, i: (b, 0), pipeline_mode=once),
                  pl.BlockSpec((S // kb, C, kb), lambda b, i: (b, 0, 0), pipeline_mode=once)],
        out_specs=pl.BlockSpec((Qb, C), lambda b, i: (b * nq + i, 0)),
        out_shape=jax.ShapeDtypeStruct((T, C), BF16),
        scratch_shapes=[pltpu.VMEM((S // kb, kb, Qb), I32),
                        pltpu.VMEM((2 * A_HEADS, Qb, LANES), BF16),
                        pltpu.VMEM((A_HEADS, kb, Qb), F32),
                        pltpu.VMEM((C, Qb), F32),
                        pltpu.VMEM((A_HEADS, 1, Qb), F32),
                        pltpu.VMEM((A_HEADS, 1, Qb), F32)],
        compiler_params=_cparams(("parallel", "arbitrary")),
        name="dsa",
    )(qi_r, q_r, misc, ki_r, k_r, v_t)


def _merge_kernel(ya_ref, yb_ref, yc_ref, ga_ref, gb_ref, gc_ref, x_ref, wb_ref, wo_ref, out_ref):
    def branch(y_ref, g_ref, n):
        return _sigmoid(g_ref[...].astype(F32)) * jnp.dot(y_ref[...], wb_ref[n], preferred_element_type=F32)

    merged = branch(ya_ref, ga_ref, 0) + branch(yb_ref, gb_ref, 1) + branch(yc_ref, gc_ref, 2)
    out_ref[...] = x_ref[...] + jnp.dot(merged.astype(BF16), wo_ref[...], preferred_element_type=F32)


def _merge(ya, yb, yc, proj, x, wb, wo, tm):
    T, D = x.shape
    C = ya.shape[1]
    row = lambda i: (i, 0)
    gate0 = 5120 // D
    return pl.pallas_call(
        _merge_kernel,
        grid=(T // tm,),
        in_specs=[pl.BlockSpec((tm, C), row), pl.BlockSpec((tm, C), row), pl.BlockSpec((tm, C), row),
                  pl.BlockSpec((tm, D), lambda i: (i, gate0)),
                  pl.BlockSpec((tm, D), lambda i: (i, gate0 + 1)),
                  pl.BlockSpec((tm, D), lambda i: (i, gate0 + 2)),
                  pl.BlockSpec((tm, D), row),
                  pl.BlockSpec((3, C, D), lambda i: (0, 0, 0)),
                  pl.BlockSpec((D, D), lambda i: (0, 0))],
        out_specs=pl.BlockSpec((tm, D), row),
        out_shape=jax.ShapeDtypeStruct((T, D), F32),
        compiler_params=_cparams(("parallel",)),
        name="merge_out",
    )(ya, yb, yc, proj, proj, proj, x, wb, wo)


def _ffn_down_kernel(g_ref, u_ref, cw_ref, cb_ref, wd_ref, x_ref, out_ref, xs_ref):
    first = pl.program_id(1) == 0
    g = _causal_conv(xs_ref, g_ref[...].astype(F32), cw_ref, cb_ref, first, FFN_CONV)
    a = _gelu(g) * u_ref[...].astype(F32)
    out_ref[...] = x_ref[...] + jnp.dot(a.astype(BF16), wd_ref[...], preferred_element_type=F32)


def _ffn_down(up, conv_w, conv_b, wd, x, B, S, tm):
    T, D = x.shape
    F = wd.shape[0]
    ns = S // tm
    return pl.pallas_call(
        _ffn_down_kernel,
        grid=(B, ns),
        in_specs=[pl.BlockSpec((tm, F), lambda b, s: (b * ns + s, 0)),
                  pl.BlockSpec((tm, F), lambda b, s: (b * ns + s, 1)),
                  pl.BlockSpec((FFN_CONV, F), lambda b, s: (0, 0)),
                  pl.BlockSpec((1, F), lambda b, s: (0, 0)),
                  pl.BlockSpec((F, D), lambda b, s: (0, 0)),
                  pl.BlockSpec((tm, D), lambda b, s: (b * ns + s, 0))],
        out_specs=pl.BlockSpec((tm, D), lambda b, s: (b * ns + s, 0)),
        out_shape=jax.ShapeDtypeStruct((T, D), F32),
        scratch_shapes=[pltpu.VMEM((tm + SUBLANES, F), F32)],
        compiler_params=_cparams(("parallel", "arbitrary")),
        name="ffn_down",
    )(up, up, conv_w, conv_b, wd, x)


def _final_norm_kernel(x_ref, g_ref, o_ref):
    x = x_ref[...]
    ms = jnp.mean(x * x, axis=-1, keepdims=True)
    o_ref[...] = x * lax.rsqrt(ms + EPS) * g_ref[...]


def _final_norm(x, g, tm):
    T, D = x.shape
    return pl.pallas_call(
        _final_norm_kernel,
        grid=(T // tm,),
        in_specs=[pl.BlockSpec((tm, D), lambda i: (i, 0)), pl.BlockSpec((1, D), lambda i: (0, 0))],
        out_specs=pl.BlockSpec((tm, D), lambda i: (i, 0)),
        out_shape=jax.ShapeDtypeStruct((T, D), F32),
        compiler_params=_cparams(("parallel",)),
        name="final_norm",
    )(x, g)


def _block_diag(w):
    n, d, e = w.shape
    eye = jnp.eye(n, dtype=w.dtype)
    return (w[:, :, None, :] * eye[:, None, :, None]).reshape(n * d, n * e)


def _split_in_proj(w_in, d_branch):
    sizes = (2 * d_branch, d_branch, d_branch, M_HEADS, M_HEADS, d_branch, d_branch, d_branch, d_branch,
             d_branch, d_branch, d_branch // IDX_HEADS, IDX_HEADS, 3 * w_in.shape[0])
    offs = np.cumsum((0,) + sizes)
    assert offs[-1] == w_in.shape[1]
    (m_qk, m_v, m_o, m_i, m_f, r_x, r_g, a_q, a_k, a_v, a_qi, a_ki, a_w, gates) = [
        w_in[:, offs[n]:offs[n + 1]] for n in range(len(sizes))]
    main = jnp.concatenate([m_qk, m_v, m_o, r_x, r_g, a_q, a_k, a_v, a_qi, gates], axis=1).astype(BF16)
    pad = jnp.zeros((w_in.shape[0], LANES - MISC_W - IDX_HEADS), w_in.dtype)
    side = jnp.concatenate([a_ki, m_i, m_f, a_w, pad], axis=1).astype(BF16)
    return main, side


def kernel(x, positions, norm_mix, w_in, mlstm_conv_w, mlstm_conv_b, mlstm_i_bias, mlstm_f_bias, mlstm_norm, rglru_conv_w, rglru_conv_b, rglru_w_a, rglru_b_a, rglru_w_x, rglru_b_x, rglru_lambda, w_branch, w_out, norm_ffn, ffn_up, ffn_conv_w, ffn_conv_b, ffn_down, norm_final):
    B, S, D = x.shape
    T = B * S
    depth = w_in.shape[0]
    d_branch = mlstm_norm.shape[1]
    d_ff = ffn_down.shape[1]
    assert D == 1024 and d_branch == 512 and d_ff == 3 * D, "column-block indexing assumes these widths"

    tm = min(1024, S)
    chunk = min(128, S)
    q_block = min(256, S)
    k_block = min(256, S)

    cos_t, sin_t = _rope_tables(positions, d_branch // A_HEADS)
    xf = x.reshape(T, D)
    for l in range(depth):
        w_main, w_side = _split_in_proj(w_in[l], d_branch)
        g_mix = norm_mix[l][None, :]
        proj = _norm_matmul(xf, g_mix, w_main, BF16, tm, 512, "in_proj")
        misc = _norm_matmul(xf, g_mix, w_side, F32, tm, LANES, "in_proj_side")

        gate_bias = jnp.zeros((1, LANES), F32)
        gate_bias = gate_bias.at[0, MISC_I:MISC_I + M_HEADS].set(mlstm_i_bias[l])
        gate_bias = gate_bias.at[0, MISC_F:MISC_F + M_HEADS].set(mlstm_f_bias[l])
        y_a = _mlstm(proj, misc, mlstm_conv_w[l], mlstm_conv_b[l][None, :], gate_bias,
                     mlstm_norm[l][None, :], B, S, chunk)

        y_b = _rglru(proj, rglru_conv_w[l], rglru_conv_b[l][None, :],
                     _block_diag(rglru_w_a[l]).astype(BF16), rglru_b_a[l][None, :],
                     _block_diag(rglru_w_x[l]).astype(BF16), rglru_b_x[l][None, :],
                     rglru_lambda[l][None, :], B, S, min(256, S))

        q_r, k_r, qi_r, ki_r, v_t = _dsa_prep(proj, misc, cos_t, sin_t, tm, k_block)
        y_c = _dsa(qi_r, q_r, misc, ki_r, k_r, v_t, B, S, q_block, k_block)

        xf = _merge(y_a, y_b, y_c, proj, xf, w_branch[l].astype(BF16), w_out[l].astype(BF16), min(512, S))

        up = _norm_matmul(xf, norm_ffn[l][None, :], ffn_up[l].astype(BF16), BF16, tm, 512, "ffn_up")
        xf = _ffn_down(up, ffn_conv_w[l], ffn_conv_b[l][None, :], ffn_down[l].astype(BF16), xf,
                       B, S, min(256, S))
    return _final_norm(xf, norm_final[None, :], tm).reshape(B, S, D)
```

```python
import functools

import jax
import jax.numpy as jnp
import numpy as np
from jax import lax
from jax.experimental import pallas as pl
from jax.experimental.pallas import tpu as pltpu

F32 = jnp.float32
BF16 = jnp.bfloat16
I32 = jnp.int32

EPS = 1e-6
ROPE_THETA = 10000.0
R_C = 8.0
TOPK_MAX = 256
INT_MIN = -(2 ** 31)
NEG_BIG = -1e30

LANES = 128
SUBLANES = 8
BF16_ROWS = 16
VMEM_LIMIT = 56 * 1024 * 1024
COUNT_CHAINS = 4
BITS_PER_CHECK = 2

M_HEADS = 4
R_BLOCKS = 8
A_HEADS = 8
IDX_HEADS = 8
M_CONV = 4
R_CONV = 4
FFN_CONV = 3

MISC_KI = 0
MISC_I = 64
MISC_F = 68
MISC_W = 72


def _cparams(sem):
    return pltpu.CompilerParams(dimension_semantics=sem, vmem_limit_bytes=VMEM_LIMIT)


def _sigmoid(x):
    return 1.0 / (1.0 + jnp.exp(-x))


def _log_sigmoid(x):
    return jnp.minimum(x, 0.0) - jnp.log1p(jnp.exp(-jnp.abs(x)))


def _expm1(y):
    p = jnp.ones_like(y)
    for n in range(13, 1, -1):
        p = 1.0 + (y * (1.0 / n)) * p
    return jnp.where(jnp.abs(y) < 0.25, y * p, jnp.exp(y) - 1.0)


def _gelu(x):
    return 0.5 * x * (1.0 + jnp.tanh(0.7978845608028654 * (x + 0.044715 * (x * x * x))))


def _tile_lanes(x, n):
    return jnp.concatenate([x] * n, axis=1)


def _causal_conv(xs_ref, x, w_ref, b_ref, first, taps):
    tm = x.shape[0]

    @pl.when(first)
    def _():
        xs_ref[0:SUBLANES, :] = jnp.zeros((SUBLANES, x.shape[1]), F32)

    xs_ref[SUBLANES:SUBLANES + tm, :] = x
    y = b_ref[...] + w_ref[taps - 1:taps, :] * x
    for j in range(taps - 1):
        d = taps - 1 - j
        y = y + w_ref[j:j + 1, :] * xs_ref[SUBLANES - d:SUBLANES - d + tm, :]
    xs_ref[0:SUBLANES, :] = xs_ref[tm:tm + SUBLANES, :]
    return y


def _rope_table_kernel(pos_ref, inv_ref, sgn_ref, cos_ref, sin_ref):
    ang = pos_ref[...] * inv_ref[...]
    cos_ref[...] = jnp.cos(ang)
    sin_ref[...] = jnp.sin(ang) * sgn_ref[...]


def _rope_tables(positions, head_dim):
    T = positions.size
    half = head_dim // 2
    inv = ROPE_THETA ** (-jnp.arange(0, head_dim, 2, dtype=F32) / head_dim)
    lane = np.arange(LANES)
    inv_row = inv[lane % half][None, :]
    sgn_row = jnp.asarray(np.where((lane % head_dim) < half, -1.0, 1.0), F32)[None, :]
    pos = positions.reshape(T, 1).astype(F32)
    tm = min(T, 1024)
    return pl.pallas_call(
        _rope_table_kernel,
        grid=(T // tm,),
        in_specs=[pl.BlockSpec((tm, 1), lambda i: (i, 0)),
                  pl.BlockSpec((1, LANES), lambda i: (0, 0)),
                  pl.BlockSpec((1, LANES), lambda i: (0, 0))],
        out_specs=[pl.BlockSpec((tm, LANES), lambda i: (i, 0)),
                   pl.BlockSpec((tm, LANES), lambda i: (i, 0))],
        out_shape=[jax.ShapeDtypeStruct((T, LANES), F32)] * 2,
        compiler_params=_cparams(("parallel",)),
        name="rope_tables",
    )(pos, inv_row, sgn_row)


def _norm_matmul_kernel(x_ref, g_ref, w_ref, o_ref, h_ref):
    @pl.when(pl.program_id(1) == 0)
    def _():
        x = x_ref[...]
        ms = jnp.mean(x * x, axis=-1, keepdims=True)
        h_ref[...] = (x * lax.rsqrt(ms + EPS) * g_ref[...]).astype(BF16)

    o_ref[...] = jnp.dot(h_ref[...], w_ref[...], preferred_element_type=F32).astype(o_ref.dtype)


def _norm_matmul(x, g, w, out_dtype, tm, tn, name):
    T, D = x.shape
    N = w.shape[1]
    return pl.pallas_call(
        _norm_matmul_kernel,
        grid=(T // tm, N // tn),
        in_specs=[pl.BlockSpec((tm, D), lambda i, j: (i, 0)),
                  pl.BlockSpec((1, D), lambda i, j: (0, 0)),
                  pl.BlockSpec((D, tn), lambda i, j: (0, j))],
        out_specs=pl.BlockSpec((tm, tn), lambda i, j: (i, j)),
        out_shape=jax.ShapeDtypeStruct((T, N), out_dtype),
        scratch_shapes=[pltpu.VMEM((tm, D), BF16)],
        compiler_params=_cparams(("parallel", "arbitrary")),
        name=name,
    )(x, g, w)


def _row_cumsum(x):
    n = x.shape[0]
    row = lax.broadcasted_iota(I32, x.shape, 0)
    d = 1
    while d < n:
        x = x + jnp.where(row >= d, pltpu.roll(x, d, axis=0), 0.0)
        d *= 2
    return x


def _mlstm_kernel(qk_ref, v_ref, o_ref, misc_ref, cw_ref, cb_ref, gb_ref, gn_ref, out_ref,
                  xs_ref, ct_ref, n_ref, m_ref, *, heads, dh):
    first = pl.program_id(1) == 0

    @pl.when(first)
    def _():
        ct_ref[...] = jnp.zeros_like(ct_ref)
        n_ref[...] = jnp.zeros_like(n_ref)
        m_ref[...] = jnp.zeros_like(m_ref)

    L = qk_ref.shape[0]
    hd = heads * dh
    qk = _causal_conv(xs_ref, qk_ref[...].astype(F32), cw_ref, cb_ref, first, M_CONV)
    qk = qk * _sigmoid(qk)

    pre = misc_ref[...] + gb_ref[...]
    g_all = _row_cumsum(_log_sigmoid(pre))
    g_t = g_all.T
    pre_t = pre.T
    row = lax.broadcasted_iota(I32, (L, L), 0)
    col = lax.broadcasted_iota(I32, (L, L), 1)
    causal = row >= col

    for h in range(heads):
        q = qk[:, h * dh:(h + 1) * dh]
        k = qk[:, hd + h * dh:hd + (h + 1) * dh] * (dh ** -0.5)
        v = v_ref[:, h * dh:(h + 1) * dh]
        g_c = g_all[:, MISC_F + h:MISC_F + h + 1]
        i_c = pre[:, MISC_I + h:MISC_I + h + 1]
        g_r = g_t[MISC_F + h:MISC_F + h + 1, :]
        i_r = pre_t[MISC_I + h:MISC_I + h + 1, :]
        m_prev = m_ref[h:h + 1, 0:1]

        logd = jnp.where(causal, g_c - g_r + i_r, -jnp.inf)
        log_inter = g_c + m_prev
        m_t = jnp.maximum(log_inter, jnp.max(logd, axis=-1, keepdims=True))
        qb = q.astype(BF16)
        kb = k.astype(BF16)
        s = lax.dot_general(qb, kb, (((1,), (1,)), ((), ())), preferred_element_type=F32)
        s = s * jnp.exp(logd - m_t)
        w_inter = jnp.exp(log_inter - m_t)
        num = jnp.dot(s.astype(BF16), v, preferred_element_type=F32)
        num = num + w_inter * jnp.dot(qb, ct_ref[h].astype(BF16), preferred_element_type=F32)
        den = jnp.sum(s, axis=-1, keepdims=True)
        den = den + w_inter * jnp.sum(q * n_ref[h:h + 1, :], axis=-1, keepdims=True)
        hh = num / jnp.maximum(jnp.abs(den), jnp.exp(-m_t))

        g_last = g_c[L - 1:L, :]
        m_new = m_t[L - 1:L, :]
        w_state = jnp.exp(g_last - g_c + i_c - m_new)
        decay = jnp.exp(g_last + m_prev - m_new)
        kw = k * w_state
        ct_ref[h] = decay * ct_ref[h] + lax.dot_general(
            kw.astype(BF16), v, (((0,), (0,)), ((), ())), preferred_element_type=F32)
        n_ref[h:h + 1, :] = decay * n_ref[h:h + 1, :] + jnp.sum(kw, axis=0, keepdims=True)
        m_ref[h:h + 1, :] = jnp.broadcast_to(m_new, (1, LANES))

        mu = jnp.mean(hh, axis=-1, keepdims=True)
        hc = hh - mu
        var = jnp.mean(hc * hc, axis=-1, keepdims=True)
        y = hc * lax.rsqrt(var + EPS) * gn_ref[:, h * dh:(h + 1) * dh]
        y = y * _sigmoid(o_ref[:, h * dh:(h + 1) * dh].astype(F32))
        out_ref[:, h * dh:(h + 1) * dh] = y.astype(out_ref.dtype)


def _mlstm(proj, misc, conv_w, conv_b, gate_bias, norm_g, B, S, L):
    T = B * S
    nc = S // L
    hd = norm_g.shape[1]
    dh = hd // M_HEADS
    kern = functools.partial(_mlstm_kernel, heads=M_HEADS, dh=dh)
    return pl.pallas_call(
        kern,
        grid=(B, nc),
        in_specs=[pl.BlockSpec((L, 2 * hd), lambda b, c: (b * nc + c, 0)),
                  pl.BlockSpec((L, hd), lambda b, c: (b * nc + c, 2)),
                  pl.BlockSpec((L, hd), lambda b, c: (b * nc + c, 3)),
                  pl.BlockSpec((L, LANES), lambda b, c: (b * nc + c, 0)),
                  pl.BlockSpec((M_CONV, 2 * hd), lambda b, c: (0, 0)),
                  pl.BlockSpec((1, 2 * hd), lambda b, c: (0, 0)),
                  pl.BlockSpec((1, LANES), lambda b, c: (0, 0)),
                  pl.BlockSpec((1, hd), lambda b, c: (0, 0))],
        out_specs=pl.BlockSpec((L, hd), lambda b, c: (b * nc + c, 0)),
        out_shape=jax.ShapeDtypeStruct((T, hd), BF16),
        scratch_shapes=[pltpu.VMEM((L + SUBLANES, 2 * hd), F32),
                        pltpu.VMEM((M_HEADS, dh, dh), F32),
                        pltpu.VMEM((SUBLANES, dh), F32),
                        pltpu.VMEM((SUBLANES, LANES), F32)],
        compiler_params=_cparams(("parallel", "arbitrary")),
        name="mlstm",
    )(proj, proj, proj, misc, conv_w, conv_b, gate_bias, norm_g)


def _rglru_kernel(x_ref, g_ref, cw_ref, cb_ref, wa_ref, ba_ref, wx_ref, bx_ref, lam_ref, out_ref,
                  xs_ref, h_ref):
    first = pl.program_id(1) == 0

    @pl.when(first)
    def _():
        h_ref[...] = jnp.zeros_like(h_ref)

    tm = x_ref.shape[0]
    x = _causal_conv(xs_ref, x_ref[...].astype(F32), cw_ref, cb_ref, first, R_CONV)
    xb = x.astype(BF16)
    r = _sigmoid(jnp.dot(xb, wa_ref[...], preferred_element_type=F32) + ba_ref[...])
    i = _sigmoid(jnp.dot(xb, wx_ref[...], preferred_element_type=F32) + bx_ref[...])
    lam = lam_ref[...]
    softplus_neg_lam = jnp.maximum(-lam, 0.0) + jnp.log1p(jnp.exp(-jnp.abs(lam)))
    log_a = -R_C * r * softplus_neg_lam
    a = jnp.exp(log_a)
    u = jnp.sqrt(-_expm1(2.0 * log_a)) * (i * x)

    row = lax.broadcasted_iota(I32, a.shape, 0)
    d = 1
    while d < tm:
        valid = row >= d
        u = jnp.where(valid, a * pltpu.roll(u, d, axis=0) + u, u)
        a = jnp.where(valid, a * pltpu.roll(a, d, axis=0), a)
        d *= 2
    hcur = a * h_ref[0:1, :] + u
    h_ref[0:1, :] = hcur[tm - 1:tm, :]
    out_ref[...] = (_gelu(g_ref[...].astype(F32)) * hcur).astype(out_ref.dtype)


def _rglru(proj, conv_w, conv_b, wa, ba, wx, bx, lam, B, S, tm):
    T = B * S
    ns = S // tm
    C = lam.shape[1]
    vec = pl.BlockSpec((1, C), lambda b, s: (0, 0))
    mat = pl.BlockSpec((C, C), lambda b, s: (0, 0))
    return pl.pallas_call(
        _rglru_kernel,
        grid=(B, ns),
        in_specs=[pl.BlockSpec((tm, C), lambda b, s: (b * ns + s, 4)),
                  pl.BlockSpec((tm, C), lambda b, s: (b * ns + s, 5)),
                  pl.BlockSpec((R_CONV, C), lambda b, s: (0, 0)), vec,
                  mat, vec, mat, vec, vec],
        out_specs=pl.BlockSpec((tm, C), lambda b, s: (b * ns + s, 0)),
        out_shape=jax.ShapeDtypeStruct((T, C), BF16),
        scratch_shapes=[pltpu.VMEM((tm + SUBLANES, C), F32),
                        pltpu.VMEM((SUBLANES, C), F32)],
        compiler_params=_cparams(("parallel", "arbitrary")),
        name="rglru",
    )(proj, proj, conv_w, conv_b, wa, ba, wx, bx, lam)


def _rope_slab(x, cos, sin_signed, first_half):
    fwd = pltpu.roll(x, 32, axis=1)
    bwd = pltpu.roll(x, 96, axis=1)
    return x * cos + jnp.where(first_half, bwd, fwd) * sin_signed


def _dsa_prep_kernel(q_ref, k_ref, qi_ref, v_ref, misc_ref, cos_ref, sin_ref,
                     qo_ref, ko_ref, qio_ref, kio_ref, vt_ref, *, q_scale, idx_scale, head_dim):
    kb = vt_ref.shape[2]
    heads = v_ref.shape[1] // head_dim
    vrows = vt_ref.shape[1] // heads
    ones = jnp.ones((vrows - head_dim, kb), BF16)
    for c in range(vt_ref.shape[0]):
        vt = v_ref[c * kb:(c + 1) * kb, :].astype(F32).T.astype(BF16)
        for h in range(heads):
            vt_ref[c, h * vrows:h * vrows + head_dim, :] = vt[h * head_dim:(h + 1) * head_dim, :]
            vt_ref[c, h * vrows + head_dim:(h + 1) * vrows, :] = ones
    tm = q_ref.shape[0]
    cos = cos_ref[...]
    sin = sin_ref[...]
    lane = lax.broadcasted_iota(I32, (tm, LANES), 1)
    first_half = (lane % head_dim) < (head_dim // 2)
    low = lane < head_dim
    misc = misc_ref[...]
    nslab = q_ref.shape[1] // LANES
    for s in range(nslab):
        sl = slice(s * LANES, (s + 1) * LANES)
        qo_ref[:, sl] = (_rope_slab(q_ref[:, sl].astype(F32), cos, sin, first_half) * q_scale).astype(BF16)
        ko_ref[:, sl] = _rope_slab(k_ref[:, sl].astype(F32), cos, sin, first_half).astype(BF16)
        w_even = jnp.abs(misc[:, MISC_W + 2 * s:MISC_W + 2 * s + 1])
        w_odd = jnp.abs(misc[:, MISC_W + 2 * s + 1:MISC_W + 2 * s + 2])
        w = jnp.where(low, w_even, w_odd) * idx_scale
        qio_ref[:, sl] = (_rope_slab(qi_ref[:, sl].astype(F32), cos, sin, first_half) * w).astype(BF16)
    ki = jnp.where(low, _rope_slab(misc, cos, sin, first_half), 0.0)
    kio_ref[:, 0:LANES] = ki.astype(BF16)
    kio_ref[:, LANES:2 * LANES] = pltpu.roll(ki, head_dim, axis=1).astype(BF16)


def _dsa_prep(proj, misc, cos_t, sin_t, tm, kb):
    T = proj.shape[0]
    C = 512
    head_dim = C // A_HEADS
    vt_rows = A_HEADS * (head_dim + BF16_ROWS)
    kern = functools.partial(
        _dsa_prep_kernel, head_dim=head_dim,
        q_scale=float(head_dim ** -0.5 * np.log2(np.e)),
        idx_scale=float(head_dim ** -0.5 * IDX_HEADS ** -0.5))
    row = lambda i: (i, 0)
    return pl.pallas_call(
        kern,
        grid=(T // tm,),
        in_specs=[pl.BlockSpec((tm, C), lambda i: (i, 6)),
                  pl.BlockSpec((tm, C), lambda i: (i, 7)),
                  pl.BlockSpec((tm, C), lambda i: (i, 9)),
                  pl.BlockSpec((tm, C), lambda i: (i, 8)),
                  pl.BlockSpec((tm, LANES), row),
                  pl.BlockSpec((tm, LANES), row),
                  pl.BlockSpec((tm, LANES), row)],
        out_specs=[pl.BlockSpec((tm, C), row), pl.BlockSpec((tm, C), row),
                   pl.BlockSpec((tm, C), row), pl.BlockSpec((tm, 2 * LANES), row),
                   pl.BlockSpec((tm // kb, vt_rows, kb), lambda i: (i, 0, 0))],
        out_shape=[jax.ShapeDtypeStruct((T, C), BF16), jax.ShapeDtypeStruct((T, C), BF16),
                   jax.ShapeDtypeStruct((T, C), BF16), jax.ShapeDtypeStruct((T, 2 * LANES), BF16),
                   jax.ShapeDtypeStruct((T // kb, vt_rows, kb), BF16)],
        compiler_params=_cparams(("parallel",)),
        name="dsa_prep",
    )(proj, proj, proj, proj, misc, cos_t, sin_t)


def _dsa_kernel(qi_ref, q_ref, misc_ref, ki_ref, k_ref, vt_ref, out_ref,
                key_ref, qm_ref, s_ref, acc_ref, m_ref, l_ref, *, top_k, kb, head_dim):
    Qb = q_ref.shape[0]
    heads = q_ref.shape[1] // head_dim
    qblk = pl.program_id(1)
    q0 = qblk * Qb
    nkb = (q0 + Qb + kb - 1) // kb
    ngrp = kb // SUBLANES
    nt = (((1,), (1,)), ((), ()))

    lane = lax.broadcasted_iota(I32, (Qb, LANES), 1)
    low = lane < head_dim
    for h in range(heads):
        sl = slice((h // 2) * LANES, (h // 2 + 1) * LANES)
        mine = low if h % 2 == 0 else jnp.logical_not(low)
        qm_ref[h] = jnp.where(mine, qi_ref[:, sl], jnp.zeros((), BF16))
        qm_ref[heads + h] = jnp.where(mine, q_ref[:, sl], jnp.zeros((), BF16))
    w_t = misc_ref[...].T
    sgn_t = jnp.where(w_t > 0, 1.0, jnp.where(w_t < 0, -1.0, 0.0))

    qpos = q0 + lax.broadcasted_iota(I32, (kb, Qb), 1)
    krow = lax.broadcasted_iota(I32, (kb, Qb), 0)
    sub = lax.broadcasted_iota(I32, (SUBLANES, Qb), 0)

    def score_block(j, carry):
        ki = ki_ref[pl.ds(pl.multiple_of(j * kb, kb), kb), :]
        for hp in range(heads // 2):
            part = None
            for h in (2 * hp, 2 * hp + 1):
                kih = ki[:, (h % 2) * LANES:(h % 2 + 1) * LANES]
                logit = lax.dot_general(kih, qm_ref[h], nt, preferred_element_type=F32)
                term = jnp.maximum(logit, 0.0) * sgn_t[MISC_W + h:MISC_W + h + 1, :]
                part = term if part is None else part + term
            if hp == 0:
                s_ref[0] = part
            else:
                s_ref[0] += part
        bits = pltpu.bitcast(s_ref[0], I32)
        bits = jnp.where(bits == INT_MIN, 0, bits)
        skey = bits ^ ((bits >> 31) & 0x7FFFFFFF)
        key_ref[j] = jnp.where(krow + j * kb <= qpos, skey, INT_MIN)
        return carry

    lax.fori_loop(0, nkb, score_block, 0)

    def count(pred):
        def body(j, cnts):
            cnts = list(cnts)
            for r in range(ngrp):
                grp = key_ref[j, r * SUBLANES:(r + 1) * SUBLANES, :]
                a = r % COUNT_CHAINS
                cnts[a] = cnts[a] + jnp.where(pred(grp, j * kb + r * SUBLANES), 1, 0)
            return tuple(cnts)
        zero = jnp.zeros((SUBLANES, Qb), I32)
        cnts = lax.fori_loop(0, nkb, body, (zero,) * COUNT_CHAINS)
        return jnp.sum(functools.reduce(lambda x, y: x + y, cnts), axis=0, keepdims=True)

    def count_ge(thr):
        thr_b = jnp.broadcast_to(thr, (SUBLANES, Qb))
        return count(lambda grp, base: grp >= thr_b)

    def bis_cond(st):
        return jnp.logical_and(st[0] >= 0, st[-1] > 0)

    def bis_body(st):
        bit, tu, c_tu, done, _ = st
        for step in range(BITS_PER_CHECK):
            cand = tu | lax.shift_left(jnp.int32(1), bit - step)
            cnt = count_ge(cand ^ INT_MIN)
            take = jnp.logical_and(cnt >= top_k, done == 0)
            tu = jnp.where(take, cand, tu)
            c_tu = jnp.where(take, cnt, c_tu)
            done = jnp.where(cnt == top_k, 1, done)
        return bit - BITS_PER_CHECK, tu, c_tu, done, jnp.max(1 - done)

    zero_row = jnp.zeros((1, Qb), I32)
    _, tu, c_thr, _, _ = lax.while_loop(
        bis_cond, bis_body, (jnp.int32(31), zero_row, zero_row, zero_row, jnp.int32(1)))
    thr = jnp.maximum(tu ^ INT_MIN, INT_MIN + 1)

    excess = jnp.logical_and(c_thr > top_k, tu != 0)
    thr_b = jnp.broadcast_to(thr, (SUBLANES, Qb))

    @pl.when(jnp.max(excess.astype(I32)) > 0)
    def _():
        need = top_k - count_ge(thr + 1)

        def count_eq_before(pos):
            pos_b = jnp.broadcast_to(pos, (SUBLANES, Qb))
            return count(lambda grp, base: jnp.logical_and(grp == thr_b, base + sub < pos_b))

        def pos_body(t, pos):
            cand = pos | lax.shift_left(jnp.int32(1), 14 - t)
            return jnp.where(count_eq_before(cand) <= need, cand, pos)

        cut = lax.fori_loop(0, 15, pos_body, jnp.zeros((1, Qb), I32))
        cut_b = jnp.broadcast_to(jnp.where(excess, cut, jnp.int32(2 ** 30)), (SUBLANES, Qb))

        def demote(j, carry):
            for r in range(ngrp):
                rows = slice(r * SUBLANES, (r + 1) * SUBLANES)
                grp = key_ref[j, rows, :]
                hit = jnp.logical_and(grp == thr_b, j * kb + r * SUBLANES + sub >= cut_b)
                key_ref[j, rows, :] = jnp.where(hit, grp - 1, grp)
            return carry

        lax.fori_loop(0, nkb, demote, 0)

    acc_ref[...] = jnp.zeros_like(acc_ref)
    m_ref[...] = jnp.full(m_ref.shape, NEG_BIG, F32)
    l_ref[...] = jnp.zeros_like(l_ref)
    thr_full = jnp.broadcast_to(thr, (kb, Qb))
    vrows = vt_ref.shape[1] // heads

    def attend_block(j, carry):
        start = pl.multiple_of(j * kb, kb)
        bias = jnp.where(key_ref[j] >= thr_full, 0.0, -jnp.inf)
        m_blk = []
        for h in range(heads):
            sl = slice((h // 2) * LANES, (h // 2 + 1) * LANES)
            s = lax.dot_general(k_ref[pl.ds(start, kb), sl], qm_ref[heads + h], nt,
                                preferred_element_type=F32) + bias
            s_ref[h] = s
            m_blk.append(jnp.max(s, axis=0, keepdims=True))
        for h in range(heads):
            dims = slice(h * head_dim, (h + 1) * head_dim)
            m_old = m_ref[h]
            m_new = jnp.maximum(m_old, m_blk[h])
            p = jnp.exp2(s_ref[h] - m_new)
            alpha = jnp.exp2(m_old - m_new)
            m_ref[h] = m_new
            pv = jnp.dot(vt_ref[j, h * vrows:(h + 1) * vrows, :], p.astype(BF16),
                         preferred_element_type=F32)
            acc_ref[dims, :] = alpha * acc_ref[dims, :] + pv[0:head_dim, :]
            l_ref[h] = alpha * l_ref[h] + pv[head_dim:head_dim + 1, :]
        return carry

    lax.fori_loop(0, nkb, attend_block, 0)

    for h in range(heads):
        dims = slice(h * head_dim, (h + 1) * head_dim)
        acc_ref[dims, :] = acc_ref[dims, :] / l_ref[h]
    out_ref[...] = acc_ref[...].T.astype(out_ref.dtype)


def _dsa(qi_r, q_r, misc, ki_r, k_r, v_t, B, S, Qb, kb):
    T = B * S
    C = q_r.shape[1]
    nq = S // Qb
    top_k = min(TOPK_MAX, S // 4)
    kern = functools.partial(_dsa_kernel, top_k=top_k, kb=kb, head_dim=C // A_HEADS)
    once = pl.Buffered(1)
    return pl.pallas_call(
        kern,
        grid=(B, nq),
        in_specs=[pl.BlockSpec((Qb, C), lambda b, i: (b * nq + i, 0)),
                  pl.BlockSpec((Qb, C), lambda b, i: (b * nq + i, 0)),
                  pl.BlockSpec((Qb, LANES), lambda b, i: (b * nq + i, 0)),
                  pl.BlockSpec((S, 2 * LANES), lambda b, i: (b, 0), pipeline_mode=once),
                  pl.BlockSpec((S, C), lambda b, i: (b, 0), pipeline_mode=once),
                  pl.BlockSpec((S // kb, v_t.shape[1], kb), lambda b, i: (b, 0, 0), pipeline_mode=once)],
        out_specs=pl.BlockSpec((Qb, C), lambda b, i: (b * nq + i, 0)),
        out_shape=jax.ShapeDtypeStruct((T, C), BF16),
        scratch_shapes=[pltpu.VMEM((S // kb, kb, Qb), I32),
                        pltpu.VMEM((2 * A_HEADS, Qb, LANES), BF16),
                        pltpu.VMEM((A_HEADS, kb, Qb), F32),
                        pltpu.VMEM((C, Qb), F32),
                        pltpu.VMEM((A_HEADS, 1, Qb), F32),
                        pltpu.VMEM((A_HEADS, 1, Qb), F32)],
        compiler_params=_cparams(("parallel", "arbitrary")),
        name="dsa",
    )(qi_r, q_r, misc, ki_r, k_r, v_t)


def _merge_kernel(ya_ref, yb_ref, yc_ref, ga_ref, gb_ref, gc_ref, x_ref, wb_ref, wo_ref, out_ref):
    def branch(y_ref, g_ref, n):
        return _sigmoid(g_ref[...].astype(F32)) * jnp.dot(y_ref[...], wb_ref[n], preferred_element_type=F32)

    merged = branch(ya_ref, ga_ref, 0) + branch(yb_ref, gb_ref, 1) + branch(yc_ref, gc_ref, 2)
    out_ref[...] = x_ref[...] + jnp.dot(merged.astype(BF16), wo_ref[...], preferred_element_type=F32)


def _merge(ya, yb, yc, proj, x, wb, wo, tm):
    T, D = x.shape
    C = ya.shape[1]
    row = lambda i: (i, 0)
    gate0 = 5120 // D
    return pl.pallas_call(
        _merge_kernel,
        grid=(T // tm,),
        in_specs=[pl.BlockSpec((tm, C), row), pl.BlockSpec((tm, C), row), pl.BlockSpec((tm, C), row),
                  pl.BlockSpec((tm, D), lambda i: (i, gate0)),
                  pl.BlockSpec((tm, D), lambda i: (i, gate0 + 1)),
                  pl.BlockSpec((tm, D), lambda i: (i, gate0 + 2)),
                  pl.BlockSpec((tm, D), row),
                  pl.BlockSpec((3, C, D), lambda i: (0, 0, 0)),
                  pl.BlockSpec((D, D), lambda i: (0, 0))],
        out_specs=pl.BlockSpec((tm, D), row),
        out_shape=jax.ShapeDtypeStruct((T, D), F32),
        compiler_params=_cparams(("parallel",)),
        name="merge_out",
    )(ya, yb, yc, proj, proj, proj, x, wb, wo)


def _ffn_down_kernel(g_ref, u_ref, cw_ref, cb_ref, wd_ref, x_ref, out_ref, xs_ref):
    first = pl.program_id(1) == 0
    g = _causal_conv(xs_ref, g_ref[...].astype(F32), cw_ref, cb_ref, first, FFN_CONV)
    a = _gelu(g) * u_ref[...].astype(F32)
    out_ref[...] = x_ref[...] + jnp.dot(a.astype(BF16), wd_ref[...], preferred_element_type=F32)


def _ffn_down(up, conv_w, conv_b, wd, x, B, S, tm):
    T, D = x.shape
    F = wd.shape[0]
    ns = S // tm
    return pl.pallas_call(
        _ffn_down_kernel,
        grid=(B, ns),
        in_specs=[pl.BlockSpec((tm, F), lambda b, s: (b * ns + s, 0)),
                  pl.BlockSpec((tm, F), lambda b, s: (b * ns + s, 1)),
                  pl.BlockSpec((FFN_CONV, F), lambda b, s: (0, 0)),
                  pl.BlockSpec((1, F), lambda b, s: (0, 0)),
                  pl.BlockSpec((F, D), lambda b, s: (0, 0)),
                  pl.BlockSpec((tm, D), lambda b, s: (b * ns + s, 0))],
        out_specs=pl.BlockSpec((tm, D), lambda b, s: (b * ns + s, 0)),
        out_shape=jax.ShapeDtypeStruct((T, D), F32),
        scratch_shapes=[pltpu.VMEM((tm + SUBLANES, F), F32)],
        compiler_params=_cparams(("parallel", "arbitrary")),
        name="ffn_down",
    )(up, up, conv_w, conv_b, wd, x)


def _final_norm_kernel(x_ref, g_ref, o_ref):
    x = x_ref[...]
    ms = jnp.mean(x * x, axis=-1, keepdims=True)
    o_ref[...] = x * lax.rsqrt(ms + EPS) * g_ref[...]


def _final_norm(x, g, tm):
    T, D = x.shape
    return pl.pallas_call(
        _final_norm_kernel,
        grid=(T // tm,),
        in_specs=[pl.BlockSpec((tm, D), lambda i: (i, 0)), pl.BlockSpec((1, D), lambda i: (0, 0))],
        out_specs=pl.BlockSpec((tm, D), lambda i: (i, 0)),
        out_shape=jax.ShapeDtypeStruct((T, D), F32),
        compiler_params=_cparams(("parallel",)),
        name="final_norm",
    )(x, g)


def _block_diag(w):
    n, d, e = w.shape
    eye = jnp.eye(n, dtype=w.dtype)
    return (w[:, :, None, :] * eye[:, None, :, None]).reshape(n * d, n * e)


def _split_in_proj(w_in, d_branch):
    sizes = (2 * d_branch, d_branch, d_branch, M_HEADS, M_HEADS, d_branch, d_branch, d_branch, d_branch,
             d_branch, d_branch, d_branch // IDX_HEADS, IDX_HEADS, 3 * w_in.shape[0])
    offs = np.cumsum((0,) + sizes)
    assert offs[-1] == w_in.shape[1]
    (m_qk, m_v, m_o, m_i, m_f, r_x, r_g, a_q, a_k, a_v, a_qi, a_ki, a_w, gates) = [
        w_in[:, offs[n]:offs[n + 1]] for n in range(len(sizes))]
    main = jnp.concatenate([m_qk, m_v, m_o, r_x, r_g, a_q, a_k, a_v, a_qi, gates], axis=1).astype(BF16)
    pad = jnp.zeros((w_in.shape[0], LANES - MISC_W - IDX_HEADS), w_in.dtype)
    side = jnp.concatenate([a_ki, m_i, m_f, a_w, pad], axis=1).astype(BF16)
    return main, side


def kernel(x, positions, norm_mix, w_in, mlstm_conv_w, mlstm_conv_b, mlstm_i_bias, mlstm_f_bias, mlstm_norm, rglru_conv_w, rglru_conv_b, rglru_w_a, rglru_b_a, rglru_w_x, rglru_b_x, rglru_lambda, w_branch, w_out, norm_ffn, ffn_up, ffn_conv_w, ffn_conv_b, ffn_down, norm_final):
    B, S, D = x.shape
    T = B * S
    depth = w_in.shape[0]
    d_branch = mlstm_norm.shape[1]
    d_ff = ffn_down.shape[1]
    assert D == 1024 and d_branch == 512 and d_ff == 3 * D, "column-block indexing assumes these widths"

    tm = min(1024, S)
    chunk = min(128, S)
    q_block = min(256, S)
    k_block = min(256, S)

    cos_t, sin_t = _rope_tables(positions, d_branch // A_HEADS)
    xf = x.reshape(T, D)
    for l in range(depth):
        w_main, w_side = _split_in_proj(w_in[l], d_branch)
        g_mix = norm_mix[l][None, :]
        proj = _norm_matmul(xf, g_mix, w_main, BF16, tm, 512, "in_proj")
        misc = _norm_matmul(xf, g_mix, w_side, F32, tm, LANES, "in_proj_side")

        gate_bias = jnp.zeros((1, LANES), F32)
        gate_bias = gate_bias.at[0, MISC_I:MISC_I + M_HEADS].set(mlstm_i_bias[l])
        gate_bias = gate_bias.at[0, MISC_F:MISC_F + M_HEADS].set(mlstm_f_bias[l])
        y_a = _mlstm(proj, misc, mlstm_conv_w[l], mlstm_conv_b[l][None, :], gate_bias,
                     mlstm_norm[l][None, :], B, S, chunk)

        y_b = _rglru(proj, rglru_conv_w[l], rglru_conv_b[l][None, :],
                     _block_diag(rglru_w_a[l]).astype(BF16), rglru_b_a[l][None, :],
                     _block_diag(rglru_w_x[l]).astype(BF16), rglru_b_x[l][None, :],
                     rglru_lambda[l][None, :], B, S, min(256, S))

        q_r, k_r, qi_r, ki_r, v_t = _dsa_prep(proj, misc, cos_t, sin_t, tm, k_block)
        y_c = _dsa(qi_r, q_r, misc, ki_r, k_r, v_t, B, S, q_block, k_block)

        xf = _merge(y_a, y_b, y_c, proj, xf, w_branch[l].astype(BF16), w_out[l].astype(BF16), min(512, S))

        up = _norm_matmul(xf, norm_ffn[l][None, :], ffn_up[l].astype(BF16), BF16, tm, 512, "ffn_up")
        xf = _ffn_down(up, ffn_conv_w[l], ffn_conv_b[l][None, :], ffn_down[l].astype(BF16), xf,
                       B, S, min(256, S))
    return _final_norm(xf, norm_final[None, :], tm).reshape(B, S, D)
```

```python
import functools

import jax
import jax.numpy as jnp
import numpy as np
from jax import lax
from jax.experimental import pallas as pl
from jax.experimental.pallas import tpu as pltpu

F32 = jnp.float32
BF16 = jnp.bfloat16
I32 = jnp.int32

EPS = 1e-6
ROPE_THETA = 10000.0
R_C = 8.0
TOPK_MAX = 256
INT_MIN = -(2 ** 31)
NEG_BIG = -1e30

LANES = 128
SUBLANES = 8
BF16_ROWS = 16
VMEM_LIMIT = 56 * 1024 * 1024
COUNT_CHAINS = 4
KEY_BITS = 32

M_HEADS = 4
R_BLOCKS = 8
A_HEADS = 8
IDX_HEADS = 8
M_CONV = 4
R_CONV = 4
FFN_CONV = 3

MISC_KI = 0
MISC_I = 64
MISC_F = 68
MISC_W = 72


def _cparams(sem):
    return pltpu.CompilerParams(dimension_semantics=sem, vmem_limit_bytes=VMEM_LIMIT)


def _sigmoid(x):
    return 1.0 / (1.0 + jnp.exp(-x))


def _log_sigmoid(x):
    return jnp.minimum(x, 0.0) - jnp.log1p(jnp.exp(-jnp.abs(x)))


def _expm1(y):
    p = jnp.ones_like(y)
    for n in range(13, 1, -1):
        p = 1.0 + (y * (1.0 / n)) * p
    return jnp.where(jnp.abs(y) < 0.25, y * p, jnp.exp(y) - 1.0)


def _gelu(x):
    return 0.5 * x * (1.0 + jnp.tanh(0.7978845608028654 * (x + 0.044715 * (x * x * x))))


def _tile_lanes(x, n):
    return jnp.concatenate([x] * n, axis=1)


def _causal_conv(xs_ref, x, w_ref, b_ref, first, taps):
    tm = x.shape[0]

    @pl.when(first)
    def _():
        xs_ref[0:SUBLANES, :] = jnp.zeros((SUBLANES, x.shape[1]), F32)

    xs_ref[SUBLANES:SUBLANES + tm, :] = x
    y = b_ref[...] + w_ref[taps - 1:taps, :] * x
    for j in range(taps - 1):
        d = taps - 1 - j
        y = y + w_ref[j:j + 1, :] * xs_ref[SUBLANES - d:SUBLANES - d + tm, :]
    xs_ref[0:SUBLANES, :] = xs_ref[tm:tm + SUBLANES, :]
    return y


def _rope_table_kernel(pos_ref, inv_ref, sgn_ref, cos_ref, sin_ref):
    ang = pos_ref[...] * inv_ref[...]
    cos_ref[...] = jnp.cos(ang)
    sin_ref[...] = jnp.sin(ang) * sgn_ref[...]


def _rope_tables(positions, head_dim):
    T = positions.size
    half = head_dim // 2
    inv = ROPE_THETA ** (-jnp.arange(0, head_dim, 2, dtype=F32) / head_dim)
    lane = np.arange(LANES)
    inv_row = inv[lane % half][None, :]
    sgn_row = jnp.asarray(np.where((lane % head_dim) < half, -1.0, 1.0), F32)[None, :]
    pos = positions.reshape(T, 1).astype(F32)
    tm = min(T, 1024)
    return pl.pallas_call(
        _rope_table_kernel,
        grid=(T // tm,),
        in_specs=[pl.BlockSpec((tm, 1), lambda i: (i, 0)),
                  pl.BlockSpec((1, LANES), lambda i: (0, 0)),
                  pl.BlockSpec((1, LANES), lambda i: (0, 0))],
        out_specs=[pl.BlockSpec((tm, LANES), lambda i: (i, 0)),
                   pl.BlockSpec((tm, LANES), lambda i: (i, 0))],
        out_shape=[jax.ShapeDtypeStruct((T, LANES), F32)] * 2,
        compiler_params=_cparams(("parallel",)),
        name="rope_tables",
    )(pos, inv_row, sgn_row)


def _norm_matmul_kernel(x_ref, g_ref, w_ref, o_ref, h_ref):
    @pl.when(pl.program_id(1) == 0)
    def _():
        x = x_ref[...]
        ms = jnp.mean(x * x, axis=-1, keepdims=True)
        h_ref[...] = (x * lax.rsqrt(ms + EPS) * g_ref[...]).astype(BF16)

    o_ref[...] = jnp.dot(h_ref[...], w_ref[...], preferred_element_type=F32).astype(o_ref.dtype)


def _norm_matmul(x, g, w, out_dtype, tm, tn, name):
    T, D = x.shape
    N = w.shape[1]
    return pl.pallas_call(
        _norm_matmul_kernel,
        grid=(T // tm, N // tn),
        in_specs=[pl.BlockSpec((tm, D), lambda i, j: (i, 0)),
                  pl.BlockSpec((1, D), lambda i, j: (0, 0)),
                  pl.BlockSpec((D, tn), lambda i, j: (0, j))],
        out_specs=pl.BlockSpec((tm, tn), lambda i, j: (i, j)),
        out_shape=jax.ShapeDtypeStruct((T, N), out_dtype),
        scratch_shapes=[pltpu.VMEM((tm, D), BF16)],
        compiler_params=_cparams(("parallel", "arbitrary")),
        name=name,
    )(x, g, w)


def _row_cumsum(x):
    n = x.shape[0]
    row = lax.broadcasted_iota(I32, x.shape, 0)
    d = 1
    while d < n:
        x = x + jnp.where(row >= d, pltpu.roll(x, d, axis=0), 0.0)
        d *= 2
    return x


def _mlstm_kernel(qk_ref, v_ref, o_ref, misc_ref, cw_ref, cb_ref, gb_ref, gn_ref, out_ref,
                  xs_ref, ct_ref, n_ref, m_ref, *, heads, dh):
    first = pl.program_id(1) == 0

    @pl.when(first)
    def _():
        ct_ref[...] = jnp.zeros_like(ct_ref)
        n_ref[...] = jnp.zeros_like(n_ref)
        m_ref[...] = jnp.zeros_like(m_ref)

    L = qk_ref.shape[0]
    hd = heads * dh
    qk = _causal_conv(xs_ref, qk_ref[...].astype(F32), cw_ref, cb_ref, first, M_CONV)
    qk = qk * _sigmoid(qk)

    pre = misc_ref[...] + gb_ref[...]
    g_all = _row_cumsum(_log_sigmoid(pre))
    g_t = g_all.T
    pre_t = pre.T
    row = lax.broadcasted_iota(I32, (L, L), 0)
    col = lax.broadcasted_iota(I32, (L, L), 1)
    causal = row >= col

    for h in range(heads):
        q = qk[:, h * dh:(h + 1) * dh]
        k = qk[:, hd + h * dh:hd + (h + 1) * dh] * (dh ** -0.5)
        v = v_ref[:, h * dh:(h + 1) * dh]
        g_c = g_all[:, MISC_F + h:MISC_F + h + 1]
        i_c = pre[:, MISC_I + h:MISC_I + h + 1]
        g_r = g_t[MISC_F + h:MISC_F + h + 1, :]
        i_r = pre_t[MISC_I + h:MISC_I + h + 1, :]
        m_prev = m_ref[h:h + 1, 0:1]

        logd = jnp.where(causal, g_c - g_r + i_r, -jnp.inf)
        log_inter = g_c + m_prev
        m_t = jnp.maximum(log_inter, jnp.max(logd, axis=-1, keepdims=True))
        qb = q.astype(BF16)
        kb = k.astype(BF16)
        s = lax.dot_general(qb, kb, (((1,), (1,)), ((), ())), preferred_element_type=F32)
        s = s * jnp.exp(logd - m_t)
        w_inter = jnp.exp(log_inter - m_t)
        num = jnp.dot(s.astype(BF16), v, preferred_element_type=F32)
        num = num + w_inter * jnp.dot(qb, ct_ref[h].astype(BF16), preferred_element_type=F32)
        den = jnp.sum(s, axis=-1, keepdims=True)
        den = den + w_inter * jnp.sum(q * n_ref[h:h + 1, :], axis=-1, keepdims=True)
        hh = num / jnp.maximum(jnp.abs(den), jnp.exp(-m_t))

        g_last = g_c[L - 1:L, :]
        m_new = m_t[L - 1:L, :]
        w_state = jnp.exp(g_last - g_c + i_c - m_new)
        decay = jnp.exp(g_last + m_prev - m_new)
        kw = k * w_state
        ct_ref[h] = decay * ct_ref[h] + lax.dot_general(
            kw.astype(BF16), v, (((0,), (0,)), ((), ())), preferred_element_type=F32)
        n_ref[h:h + 1, :] = decay * n_ref[h:h + 1, :] + jnp.sum(kw, axis=0, keepdims=True)
        m_ref[h:h + 1, :] = jnp.broadcast_to(m_new, (1, LANES))

        mu = jnp.mean(hh, axis=-1, keepdims=True)
        hc = hh - mu
        var = jnp.mean(hc * hc, axis=-1, keepdims=True)
        y = hc * lax.rsqrt(var + EPS) * gn_ref[:, h * dh:(h + 1) * dh]
        y = y * _sigmoid(o_ref[:, h * dh:(h + 1) * dh].astype(F32))
        out_ref[:, h * dh:(h + 1) * dh] = y.astype(out_ref.dtype)


def _mlstm(proj, misc, conv_w, conv_b, gate_bias, norm_g, B, S, L):
    T = B * S
    nc = S // L
    hd = norm_g.shape[1]
    dh = hd // M_HEADS
    kern = functools.partial(_mlstm_kernel, heads=M_HEADS, dh=dh)
    return pl.pallas_call(
        kern,
        grid=(B, nc),
        in_specs=[pl.BlockSpec((L, 2 * hd), lambda b, c: (b * nc + c, 0)),
                  pl.BlockSpec((L, hd), lambda b, c: (b * nc + c, 2)),
                  pl.BlockSpec((L, hd), lambda b, c: (b * nc + c, 3)),
                  pl.BlockSpec((L, LANES), lambda b, c: (b * nc + c, 0)),
                  pl.BlockSpec((M_CONV, 2 * hd), lambda b, c: (0, 0)),
                  pl.BlockSpec((1, 2 * hd), lambda b, c: (0, 0)),
                  pl.BlockSpec((1, LANES), lambda b, c: (0, 0)),
                  pl.BlockSpec((1, hd), lambda b, c: (0, 0))],
        out_specs=pl.BlockSpec((L, hd), lambda b, c: (b * nc + c, 0)),
        out_shape=jax.ShapeDtypeStruct((T, hd), BF16),
        scratch_shapes=[pltpu.VMEM((L + SUBLANES, 2 * hd), F32),
                        pltpu.VMEM((M_HEADS, dh, dh), F32),
                        pltpu.VMEM((SUBLANES, dh), F32),
                        pltpu.VMEM((SUBLANES, LANES), F32)],
        compiler_params=_cparams(("parallel", "arbitrary")),
        name="mlstm",
    )(proj, proj, proj, misc, conv_w, conv_b, gate_bias, norm_g)


def _rglru_kernel(x_ref, g_ref, cw_ref, cb_ref, wa_ref, ba_ref, wx_ref, bx_ref, lam_ref, out_ref,
                  xs_ref, h_ref):
    first = pl.program_id(1) == 0

    @pl.when(first)
    def _():
        h_ref[...] = jnp.zeros_like(h_ref)

    tm = x_ref.shape[0]
    x = _causal_conv(xs_ref, x_ref[...].astype(F32), cw_ref, cb_ref, first, R_CONV)
    xb = x.astype(BF16)
    r = _sigmoid(jnp.dot(xb, wa_ref[...], preferred_element_type=F32) + ba_ref[...])
    i = _sigmoid(jnp.dot(xb, wx_ref[...], preferred_element_type=F32) + bx_ref[...])
    lam = lam_ref[...]
    softplus_neg_lam = jnp.maximum(-lam, 0.0) + jnp.log1p(jnp.exp(-jnp.abs(lam)))
    log_a = -R_C * r * softplus_neg_lam
    a = jnp.exp(log_a)
    u = jnp.sqrt(-_expm1(2.0 * log_a)) * (i * x)

    row = lax.broadcasted_iota(I32, a.shape, 0)
    d = 1
    while d < tm:
        valid = row >= d
        u = jnp.where(valid, a * pltpu.roll(u, d, axis=0) + u, u)
        a = jnp.where(valid, a * pltpu.roll(a, d, axis=0), a)
        d *= 2
    hcur = a * h_ref[0:1, :] + u
    h_ref[0:1, :] = hcur[tm - 1:tm, :]
    out_ref[...] = (_gelu(g_ref[...].astype(F32)) * hcur).astype(out_ref.dtype)


def _rglru(proj, conv_w, conv_b, wa, ba, wx, bx, lam, B, S, tm):
    T = B * S
    ns = S // tm
    C = lam.shape[1]
    vec = pl.BlockSpec((1, C), lambda b, s: (0, 0))
    mat = pl.BlockSpec((C, C), lambda b, s: (0, 0))
    return pl.pallas_call(
        _rglru_kernel,
        grid=(B, ns),
        in_specs=[pl.BlockSpec((tm, C), lambda b, s: (b * ns + s, 4)),
                  pl.BlockSpec((tm, C), lambda b, s: (b * ns + s, 5)),
                  pl.BlockSpec((R_CONV, C), lambda b, s: (0, 0)), vec,
                  mat, vec, mat, vec, vec],
        out_specs=pl.BlockSpec((tm, C), lambda b, s: (b * ns + s, 0)),
        out_shape=jax.ShapeDtypeStruct((T, C), BF16),
        scratch_shapes=[pltpu.VMEM((tm + SUBLANES, C), F32),
                        pltpu.VMEM((SUBLANES, C), F32)],
        compiler_params=_cparams(("parallel", "arbitrary")),
        name="rglru",
    )(proj, proj, conv_w, conv_b, wa, ba, wx, bx, lam)


def _rope_slab(x, cos, sin_signed, first_half):
    fwd = pltpu.roll(x, 32, axis=1)
    bwd = pltpu.roll(x, 96, axis=1)
    return x * cos + jnp.where(first_half, bwd, fwd) * sin_signed


def _dsa_prep_kernel(q_ref, k_ref, qi_ref, v_ref, misc_ref, cos_ref, sin_ref,
                     qo_ref, ko_ref, qio_ref, kio_ref, vt_ref, *, q_scale, idx_scale, head_dim):
    kb = vt_ref.shape[2]
    heads = v_ref.shape[1] // head_dim
    vrows = vt_ref.shape[1] // heads
    ones = jnp.ones((vrows - head_dim, kb), BF16)
    for c in range(vt_ref.shape[0]):
        vt = v_ref[c * kb:(c + 1) * kb, :].astype(F32).T.astype(BF16)
        for h in range(heads):
            vt_ref[c, h * vrows:h * vrows + head_dim, :] = vt[h * head_dim:(h + 1) * head_dim, :]
            vt_ref[c, h * vrows + head_dim:(h + 1) * vrows, :] = ones
    tm = q_ref.shape[0]
    cos = cos_ref[...]
    sin = sin_ref[...]
    lane = lax.broadcasted_iota(I32, (tm, LANES), 1)
    first_half = (lane % head_dim) < (head_dim // 2)
    low = lane < head_dim
    misc = misc_ref[...]
    nslab = q_ref.shape[1] // LANES
    for s in range(nslab):
        sl = slice(s * LANES, (s + 1) * LANES)
        qo_ref[:, sl] = (_rope_slab(q_ref[:, sl].astype(F32), cos, sin, first_half) * q_scale).astype(BF16)
        ko_ref[:, sl] = _rope_slab(k_ref[:, sl].astype(F32), cos, sin, first_half).astype(BF16)
        w_even = jnp.abs(misc[:, MISC_W + 2 * s:MISC_W + 2 * s + 1])
        w_odd = jnp.abs(misc[:, MISC_W + 2 * s + 1:MISC_W + 2 * s + 2])
        w = jnp.where(low, w_even, w_odd) * idx_scale
        qio_ref[:, sl] = (_rope_slab(qi_ref[:, sl].astype(F32), cos, sin, first_half) * w).astype(BF16)
    ki = jnp.where(low, _rope_slab(misc, cos, sin, first_half), 0.0)
    kio_ref[:, 0:LANES] = ki.astype(BF16)
    kio_ref[:, LANES:2 * LANES] = pltpu.roll(ki, head_dim, axis=1).astype(BF16)


def _dsa_prep(proj, misc, cos_t, sin_t, tm, kb):
    T = proj.shape[0]
    C = 512
    head_dim = C // A_HEADS
    vt_rows = A_HEADS * (head_dim + BF16_ROWS)
    kern = functools.partial(
        _dsa_prep_kernel, head_dim=head_dim,
        q_scale=float(head_dim ** -0.5 * np.log2(np.e)),
        idx_scale=float(head_dim ** -0.5 * IDX_HEADS ** -0.5))
    row = lambda i: (i, 0)
    return pl.pallas_call(
        kern,
        grid=(T // tm,),
        in_specs=[pl.BlockSpec((tm, C), lambda i: (i, 6)),
                  pl.BlockSpec((tm, C), lambda i: (i, 7)),
                  pl.BlockSpec((tm, C), lambda i: (i, 9)),
                  pl.BlockSpec((tm, C), lambda i: (i, 8)),
                  pl.BlockSpec((tm, LANES), row),
                  pl.BlockSpec((tm, LANES), row),
                  pl.BlockSpec((tm, LANES), row)],
        out_specs=[pl.BlockSpec((tm, C), row), pl.BlockSpec((tm, C), row),
                   pl.BlockSpec((tm, C), row), pl.BlockSpec((tm, 2 * LANES), row),
                   pl.BlockSpec((tm // kb, vt_rows, kb), lambda i: (i, 0, 0))],
        out_shape=[jax.ShapeDtypeStruct((T, C), BF16), jax.ShapeDtypeStruct((T, C), BF16),
                   jax.ShapeDtypeStruct((T, C), BF16), jax.ShapeDtypeStruct((T, 2 * LANES), BF16),
                   jax.ShapeDtypeStruct((T // kb, vt_rows, kb), BF16)],
        compiler_params=_cparams(("parallel",)),
        name="dsa_prep",
    )(proj, proj, proj, proj, misc, cos_t, sin_t)


def _bit_transpose(words):
    a = list(words)
    j, m = 16, 0x0000FFFF
    while j:
        for k in range(KEY_BITS):
            if k & j == 0:
                t = (a[k] ^ (a[k + j] >> j)) & m
                a[k] = a[k] ^ t
                a[k + j] = a[k + j] ^ (t << j)
        j >>= 1
        m = (m ^ (m << j)) & 0xFFFFFFFF
    return a


def _dsa_kernel(qi_ref, q_ref, misc_ref, ki_ref, k_ref, vt_ref, out_ref,
                key_ref, plane_ref, alive_ref, qm_ref, s_ref, acc_ref, m_ref, l_ref, *, top_k, kb, head_dim):
    Qb = q_ref.shape[0]
    heads = q_ref.shape[1] // head_dim
    qblk = pl.program_id(1)
    q0 = qblk * Qb
    nkb = (q0 + Qb + kb - 1) // kb
    ngrp = kb // SUBLANES
    nt = (((1,), (1,)), ((), ()))

    lane = lax.broadcasted_iota(I32, (Qb, LANES), 1)
    low = lane < head_dim
    for h in range(heads):
        sl = slice((h // 2) * LANES, (h // 2 + 1) * LANES)
        mine = low if h % 2 == 0 else jnp.logical_not(low)
        qm_ref[h] = jnp.where(mine, qi_ref[:, sl], jnp.zeros((), BF16))
        qm_ref[heads + h] = jnp.where(mine, q_ref[:, sl], jnp.zeros((), BF16))
    w_t = misc_ref[...].T
    sgn_t = jnp.where(w_t > 0, 1.0, jnp.where(w_t < 0, -1.0, 0.0))

    qpos = q0 + lax.broadcasted_iota(I32, (kb, Qb), 1)
    krow = lax.broadcasted_iota(I32, (kb, Qb), 0)
    sub = lax.broadcasted_iota(I32, (SUBLANES, Qb), 0)

    def score_block(j, carry):
        ki = ki_ref[pl.ds(pl.multiple_of(j * kb, kb), kb), :]
        for hp in range(heads // 2):
            part = None
            for h in (2 * hp, 2 * hp + 1):
                kih = ki[:, (h % 2) * LANES:(h % 2 + 1) * LANES]
                logit = lax.dot_general(kih, qm_ref[h], nt, preferred_element_type=F32)
                term = jnp.maximum(logit, 0.0) * sgn_t[MISC_W + h:MISC_W + h + 1, :]
                part = term if part is None else part + term
            if hp == 0:
                s_ref[0] = part
            else:
                s_ref[0] += part
        bits = pltpu.bitcast(s_ref[0], I32)
        bits = jnp.where(bits == INT_MIN, 0, bits)
        skey = bits ^ ((bits >> 31) & 0x7FFFFFFF)
        key_ref[j] = jnp.where(krow + j * kb <= qpos, skey, INT_MIN)
        for half in range(Qb // LANES):
            cols = slice(half * LANES, (half + 1) * LANES)
            words = [key_ref[j, i * SUBLANES:(i + 1) * SUBLANES, cols] ^ INT_MIN for i in range(KEY_BITS)]
            for t, plane in enumerate(_bit_transpose(words)):
                plane_ref[j, t, :, cols] = plane
        alive_ref[j] = jnp.full((SUBLANES, Qb), -1, I32)
        return carry

    lax.fori_loop(0, nkb, score_block, 0)

    def plane_step(t, above, tu, flip, first):
        def body(j, acc):
            alive = alive_ref[j]
            if not first:
                alive = alive & (plane_ref[j, t - 1] ^ flip)
                alive_ref[j] = alive
            return acc + lax.population_count(alive & plane_ref[j, t])
        ones = lax.fori_loop(0, nkb, body, jnp.zeros((SUBLANES, Qb), I32))
        ones = jnp.sum(ones, axis=0, keepdims=True)
        take = above + ones >= top_k
        tu = tu | jnp.where(take, lax.shift_left(jnp.int32(1), KEY_BITS - 1 - t), 0)
        above = jnp.where(take, above, above + ones)
        flip = jnp.broadcast_to(jnp.where(take, 0, -1), (SUBLANES, Qb))
        return above, tu, flip

    zero_row = jnp.zeros((1, Qb), I32)
    st = plane_step(0, zero_row, zero_row, None, True)
    above, tu, flip = lax.fori_loop(1, KEY_BITS, lambda t, st: plane_step(t, *st, False), st)

    def tie_body(j, acc):
        return acc + lax.population_count(alive_ref[j] & (plane_ref[j, KEY_BITS - 1] ^ flip))
    ties = lax.fori_loop(0, nkb, tie_body, jnp.zeros((SUBLANES, Qb), I32))
    c_thr = above + jnp.sum(ties, axis=0, keepdims=True)
    thr = jnp.maximum(tu ^ INT_MIN, INT_MIN + 1)

    def count(pred):
        def body(j, cnts):
            cnts = list(cnts)
            for r in range(ngrp):
                grp = key_ref[j, r * SUBLANES:(r + 1) * SUBLANES, :]
                a = r % COUNT_CHAINS
                cnts[a] = cnts[a] + jnp.where(pred(grp, j * kb + r * SUBLANES), 1, 0)
            return tuple(cnts)
        zero = jnp.zeros((SUBLANES, Qb), I32)
        cnts = lax.fori_loop(0, nkb, body, (zero,) * COUNT_CHAINS)
        return jnp.sum(functools.reduce(lambda x, y: x + y, cnts), axis=0, keepdims=True)

    def count_ge(thr):
        thr_b = jnp.broadcast_to(thr, (SUBLANES, Qb))
        return count(lambda grp, base: grp >= thr_b)

    excess = jnp.logical_and(c_thr > top_k, tu != 0)
    thr_b = jnp.broadcast_to(thr, (SUBLANES, Qb))

    @pl.when(jnp.max(excess.astype(I32)) > 0)
    def _():
        need = top_k - count_ge(thr + 1)

        def count_eq_before(pos):
            pos_b = jnp.broadcast_to(pos, (SUBLANES, Qb))
            return count(lambda grp, base: jnp.logical_and(grp == thr_b, base + sub < pos_b))

        def pos_body(t, pos):
            cand = pos | lax.shift_left(jnp.int32(1), 14 - t)
            return jnp.where(count_eq_before(cand) <= need, cand, pos)

        cut = lax.fori_loop(0, 15, pos_body, jnp.zeros((1, Qb), I32))
        cut_b = jnp.broadcast_to(jnp.where(excess, cut, jnp.int32(2 ** 30)), (SUBLANES, Qb))

        def demote(j, carry):
            for r in range(ngrp):
                rows = slice(r * SUBLANES, (r + 1) * SUBLANES)
                grp = key_ref[j, rows, :]
                hit = jnp.logical_and(grp == thr_b, j * kb + r * SUBLANES + sub >= cut_b)
                key_ref[j, rows, :] = jnp.where(hit, grp - 1, grp)
            return carry

        lax.fori_loop(0, nkb, demote, 0)

    acc_ref[...] = jnp.zeros_like(acc_ref)
    m_ref[...] = jnp.full(m_ref.shape, NEG_BIG, F32)
    l_ref[...] = jnp.zeros_like(l_ref)
    thr_full = jnp.broadcast_to(thr, (kb, Qb))
    vrows = vt_ref.shape[1] // heads

    def attend_block(j, carry):
        start = pl.multiple_of(j * kb, kb)
        bias = jnp.where(key_ref[j] >= thr_full, 0.0, -jnp.inf)
        m_blk = []
        for h in range(heads):
            sl = slice((h // 2) * LANES, (h // 2 + 1) * LANES)
            s = lax.dot_general(k_ref[pl.ds(start, kb), sl], qm_ref[heads + h], nt,
                                preferred_element_type=F32) + bias
            s_ref[h] = s
            m_blk.append(jnp.max(s, axis=0, keepdims=True))
        for h in range(heads):
            dims = slice(h * head_dim, (h + 1) * head_dim)
            m_old = m_ref[h]
            m_new = jnp.maximum(m_old, m_blk[h])
            p = jnp.exp2(s_ref[h] - m_new)
            alpha = jnp.exp2(m_old - m_new)
            m_ref[h] = m_new
            pv = jnp.dot(vt_ref[j, h * vrows:(h + 1) * vrows, :], p.astype(BF16),
                         preferred_element_type=F32)
            acc_ref[dims, :] = alpha * acc_ref[dims, :] + pv[0:head_dim, :]
            l_ref[h] = alpha * l_ref[h] + pv[head_dim:head_dim + 1, :]
        return carry

    lax.fori_loop(0, nkb, attend_block, 0)

    for h in range(heads):
        dims = slice(h * head_dim, (h + 1) * head_dim)
        acc_ref[dims, :] = acc_ref[dims, :] / l_ref[h]
    out_ref[...] = acc_ref[...].T.astype(out_ref.dtype)


def _dsa(qi_r, q_r, misc, ki_r, k_r, v_t, B, S, Qb, kb):
    T = B * S
    C = q_r.shape[1]
    nq = S // Qb
    top_k = min(TOPK_MAX, S // 4)
    assert kb == KEY_BITS * SUBLANES, "one bit-plane word packs one key per sublane group of a key block"
    kern = functools.partial(_dsa_kernel, top_k=top_k, kb=kb, head_dim=C // A_HEADS)
    once = pl.Buffered(1)
    return pl.pallas_call(
        kern,
        grid=(B, nq),
        in_specs=[pl.BlockSpec((Qb, C), lambda b, i: (b * nq + i, 0)),
                  pl.BlockSpec((Qb, C), lambda b, i: (b * nq + i, 0)),
                  pl.BlockSpec((Qb, LANES), lambda b, i: (b * nq + i, 0)),
                  pl.BlockSpec((S, 2 * LANES), lambda b, i: (b, 0), pipeline_mode=once),
                  pl.BlockSpec((S, C), lambda b, i: (b, 0), pipeline_mode=once),
                  pl.BlockSpec((S // kb, v_t.shape[1], kb), lambda b, i: (b, 0, 0), pipeline_mode=once)],
        out_specs=pl.BlockSpec((Qb, C), lambda b, i: (b * nq + i, 0)),
        out_shape=jax.ShapeDtypeStruct((T, C), BF16),
        scratch_shapes=[pltpu.VMEM((S // kb, kb, Qb), I32),
                        pltpu.VMEM((S // kb, KEY_BITS, SUBLANES, Qb), I32),
                        pltpu.VMEM((S // kb, SUBLANES, Qb), I32),
                        pltpu.VMEM((2 * A_HEADS, Qb, LANES), BF16),
                        pltpu.VMEM((A_HEADS, kb, Qb), F32),
                        pltpu.VMEM((C, Qb), F32),
                        pltpu.VMEM((A_HEADS, 1, Qb), F32),
                        pltpu.VMEM((A_HEADS, 1, Qb), F32)],
        compiler_params=_cparams(("parallel", "arbitrary")),
        name="dsa",
    )(qi_r, q_r, misc, ki_r, k_r, v_t)


def _merge_kernel(ya_ref, yb_ref, yc_ref, ga_ref, gb_ref, gc_ref, x_ref, wb_ref, wo_ref, out_ref):
    def branch(y_ref, g_ref, n):
        return _sigmoid(g_ref[...].astype(F32)) * jnp.dot(y_ref[...], wb_ref[n], preferred_element_type=F32)

    merged = branch(ya_ref, ga_ref, 0) + branch(yb_ref, gb_ref, 1) + branch(yc_ref, gc_ref, 2)
    out_ref[...] = x_ref[...] + jnp.dot(merged.astype(BF16), wo_ref[...], preferred_element_type=F32)


def _merge(ya, yb, yc, proj, x, wb, wo, tm):
    T, D = x.shape
    C = ya.shape[1]
    row = lambda i: (i, 0)
    gate0 = 5120 // D
    return pl.pallas_call(
        _merge_kernel,
        grid=(T // tm,),
        in_specs=[pl.BlockSpec((tm, C), row), pl.BlockSpec((tm, C), row), pl.BlockSpec((tm, C), row),
                  pl.BlockSpec((tm, D), lambda i: (i, gate0)),
                  pl.BlockSpec((tm, D), lambda i: (i, gate0 + 1)),
                  pl.BlockSpec((tm, D), lambda i: (i, gate0 + 2)),
                  pl.BlockSpec((tm, D), row),
                  pl.BlockSpec((3, C, D), lambda i: (0, 0, 0)),
                  pl.BlockSpec((D, D), lambda i: (0, 0))],
        out_specs=pl.BlockSpec((tm, D), row),
        out_shape=jax.ShapeDtypeStruct((T, D), F32),
        compiler_params=_cparams(("parallel",)),
        name="merge_out",
    )(ya, yb, yc, proj, proj, proj, x, wb, wo)


def _ffn_down_kernel(g_ref, u_ref, cw_ref, cb_ref, wd_ref, x_ref, out_ref, xs_ref):
    first = pl.program_id(1) == 0
    g = _causal_conv(xs_ref, g_ref[...].astype(F32), cw_ref, cb_ref, first, FFN_CONV)
    a = _gelu(g) * u_ref[...].astype(F32)
    out_ref[...] = x_ref[...] + jnp.dot(a.astype(BF16), wd_ref[...], preferred_element_type=F32)


def _ffn_down(up, conv_w, conv_b, wd, x, B, S, tm):
    T, D = x.shape
    F = wd.shape[0]
    ns = S // tm
    return pl.pallas_call(
        _ffn_down_kernel,
        grid=(B, ns),
        in_specs=[pl.BlockSpec((tm, F), lambda b, s: (b * ns + s, 0)),
                  pl.BlockSpec((tm, F), lambda b, s: (b * ns + s, 1)),
                  pl.BlockSpec((FFN_CONV, F), lambda b, s: (0, 0)),
                  pl.BlockSpec((1, F), lambda b, s: (0, 0)),
                  pl.BlockSpec((F, D), lambda b, s: (0, 0)),
                  pl.BlockSpec((tm, D), lambda b, s: (b * ns + s, 0))],
        out_specs=pl.BlockSpec((tm, D), lambda b, s: (b * ns + s, 0)),
        out_shape=jax.ShapeDtypeStruct((T, D), F32),
        scratch_shapes=[pltpu.VMEM((tm + SUBLANES, F), F32)],
        compiler_params=_cparams(("parallel", "arbitrary")),
        name="ffn_down",
    )(up, up, conv_w, conv_b, wd, x)


def _final_norm_kernel(x_ref, g_ref, o_ref):
    x = x_ref[...]
    ms = jnp.mean(x * x, axis=-1, keepdims=True)
    o_ref[...] = x * lax.rsqrt(ms + EPS) * g_ref[...]


def _final_norm(x, g, tm):
    T, D = x.shape
    return pl.pallas_call(
        _final_norm_kernel,
        grid=(T // tm,),
        in_specs=[pl.BlockSpec((tm, D), lambda i: (i, 0)), pl.BlockSpec((1, D), lambda i: (0, 0))],
        out_specs=pl.BlockSpec((tm, D), lambda i: (i, 0)),
        out_shape=jax.ShapeDtypeStruct((T, D), F32),
        compiler_params=_cparams(("parallel",)),
        name="final_norm",
    )(x, g)


def _block_diag(w):
    n, d, e = w.shape
    eye = jnp.eye(n, dtype=w.dtype)
    return (w[:, :, None, :] * eye[:, None, :, None]).reshape(n * d, n * e)


def _split_in_proj(w_in, d_branch):
    sizes = (2 * d_branch, d_branch, d_branch, M_HEADS, M_HEADS, d_branch, d_branch, d_branch, d_branch,
             d_branch, d_branch, d_branch // IDX_HEADS, IDX_HEADS, 3 * w_in.shape[0])
    offs = np.cumsum((0,) + sizes)
    assert offs[-1] == w_in.shape[1]
    (m_qk, m_v, m_o, m_i, m_f, r_x, r_g, a_q, a_k, a_v, a_qi, a_ki, a_w, gates) = [
        w_in[:, offs[n]:offs[n + 1]] for n in range(len(sizes))]
    main = jnp.concatenate([m_qk, m_v, m_o, r_x, r_g, a_q, a_k, a_v, a_qi, gates], axis=1).astype(BF16)
    pad = jnp.zeros((w_in.shape[0], LANES - MISC_W - IDX_HEADS), w_in.dtype)
    side = jnp.concatenate([a_ki, m_i, m_f, a_w, pad], axis=1).astype(BF16)
    return main, side


def kernel(x, positions, norm_mix, w_in, mlstm_conv_w, mlstm_conv_b, mlstm_i_bias, mlstm_f_bias, mlstm_norm, rglru_conv_w, rglru_conv_b, rglru_w_a, rglru_b_a, rglru_w_x, rglru_b_x, rglru_lambda, w_branch, w_out, norm_ffn, ffn_up, ffn_conv_w, ffn_conv_b, ffn_down, norm_final):
    B, S, D = x.shape
    T = B * S
    depth = w_in.shape[0]
    d_branch = mlstm_norm.shape[1]
    d_ff = ffn_down.shape[1]
    assert D == 1024 and d_branch == 512 and d_ff == 3 * D, "column-block indexing assumes these widths"

    tm = min(1024, S)
    chunk = min(128, S)
    q_block = min(256, S)
    k_block = min(256, S)

    cos_t, sin_t = _rope_tables(positions, d_branch // A_HEADS)
    xf = x.reshape(T, D)
    for l in range(depth):
        w_main, w_side = _split_in_proj(w_in[l], d_branch)
        g_mix = norm_mix[l][None, :]
        proj = _norm_matmul(xf, g_mix, w_main, BF16, tm, 1024, "in_proj")
        misc = _norm_matmul(xf, g_mix, w_side, F32, tm, LANES, "in_proj_side")

        gate_bias = jnp.zeros((1, LANES), F32)
        gate_bias = gate_bias.at[0, MISC_I:MISC_I + M_HEADS].set(mlstm_i_bias[l])
        gate_bias = gate_bias.at[0, MISC_F:MISC_F + M_HEADS].set(mlstm_f_bias[l])
        y_a = _mlstm(proj, misc, mlstm_conv_w[l], mlstm_conv_b[l][None, :], gate_bias,
                     mlstm_norm[l][None, :], B, S, chunk)

        y_b = _rglru(proj, rglru_conv_w[l], rglru_conv_b[l][None, :],
                     _block_diag(rglru_w_a[l]).astype(BF16), rglru_b_a[l][None, :],
                     _block_diag(rglru_w_x[l]).astype(BF16), rglru_b_x[l][None, :],
                     rglru_lambda[l][None, :], B, S, min(256, S))

        q_r, k_r, qi_r, ki_r, v_t = _dsa_prep(proj, misc, cos_t, sin_t, tm, k_block)
        y_c = _dsa(qi_r, q_r, misc, ki_r, k_r, v_t, B, S, q_block, k_block)

        xf = _merge(y_a, y_b, y_c, proj, xf, w_branch[l].astype(BF16), w_out[l].astype(BF16), min(512, S))

        up = _norm_matmul(xf, norm_ffn[l][None, :], ffn_up[l].astype(BF16), BF16, tm, 1024, "ffn_up")
        xf = _ffn_down(up, ffn_conv_w[l], ffn_conv_b[l][None, :], ffn_down[l].astype(BF16), xf,
                       B, S, min(256, S))
    return _final_norm(xf, norm_final[None, :], tm).reshape(B, S, D)
```

```python
import functools

import jax
import jax.numpy as jnp
import numpy as np
from jax import lax
from jax.experimental import pallas as pl
from jax.experimental.pallas import tpu as pltpu

F32 = jnp.float32
BF16 = jnp.bfloat16
I32 = jnp.int32

EPS = 1e-6
ROPE_THETA = 10000.0
R_C = 8.0
TOPK_MAX = 256
INT_MIN = -(2 ** 31)
NEG_BIG = -1e30

LANES = 128
SUBLANES = 8
BF16_ROWS = 16
VMEM_LIMIT = 56 * 1024 * 1024
COUNT_CHAINS = 4
KEY_BITS = 32

M_HEADS = 4
R_BLOCKS = 8
A_HEADS = 8
IDX_HEADS = 8
M_CONV = 4
R_CONV = 4
FFN_CONV = 3

MISC_KI = 0
MISC_I = 64
MISC_F = 68
MISC_W = 72


def _cparams(sem):
    return pltpu.CompilerParams(dimension_semantics=sem, vmem_limit_bytes=VMEM_LIMIT)


def _sigmoid(x):
    return 1.0 / (1.0 + jnp.exp(-x))


def _log_sigmoid(x):
    return jnp.minimum(x, 0.0) - jnp.log1p(jnp.exp(-jnp.abs(x)))


def _expm1(y):
    p = jnp.ones_like(y)
    for n in range(13, 1, -1):
        p = 1.0 + (y * (1.0 / n)) * p
    return jnp.where(jnp.abs(y) < 0.25, y * p, jnp.exp(y) - 1.0)


def _gelu(x):
    return 0.5 * x * (1.0 + jnp.tanh(0.7978845608028654 * (x + 0.044715 * (x * x * x))))


def _tile_lanes(x, n):
    return jnp.concatenate([x] * n, axis=1)


def _causal_conv(xs_ref, x, w_ref, b_ref, first, taps):
    tm = x.shape[0]

    @pl.when(first)
    def _():
        xs_ref[0:SUBLANES, :] = jnp.zeros((SUBLANES, x.shape[1]), F32)

    xs_ref[SUBLANES:SUBLANES + tm, :] = x
    y = b_ref[...] + w_ref[taps - 1:taps, :] * x
    for j in range(taps - 1):
        d = taps - 1 - j
        y = y + w_ref[j:j + 1, :] * xs_ref[SUBLANES - d:SUBLANES - d + tm, :]
    xs_ref[0:SUBLANES, :] = xs_ref[tm:tm + SUBLANES, :]
    return y


def _rope_table_kernel(pos_ref, inv_ref, sgn_ref, cos_ref, sin_ref):
    ang = pos_ref[...] * inv_ref[...]
    cos_ref[...] = jnp.cos(ang)
    sin_ref[...] = jnp.sin(ang) * sgn_ref[...]


def _rope_tables(positions, head_dim):
    T = positions.size
    half = head_dim // 2
    inv = ROPE_THETA ** (-jnp.arange(0, head_dim, 2, dtype=F32) / head_dim)
    lane = np.arange(LANES)
    inv_row = inv[lane % half][None, :]
    sgn_row = jnp.asarray(np.where((lane % head_dim) < half, -1.0, 1.0), F32)[None, :]
    pos = positions.reshape(T, 1).astype(F32)
    tm = min(T, 1024)
    return pl.pallas_call(
        _rope_table_kernel,
        grid=(T // tm,),
        in_specs=[pl.BlockSpec((tm, 1), lambda i: (i, 0)),
                  pl.BlockSpec((1, LANES), lambda i: (0, 0)),
                  pl.BlockSpec((1, LANES), lambda i: (0, 0))],
        out_specs=[pl.BlockSpec((tm, LANES), lambda i: (i, 0)),
                   pl.BlockSpec((tm, LANES), lambda i: (i, 0))],
        out_shape=[jax.ShapeDtypeStruct((T, LANES), F32)] * 2,
        compiler_params=_cparams(("parallel",)),
        name="rope_tables",
    )(pos, inv_row, sgn_row)


def _norm_matmul_kernel(x_ref, g_ref, w_ref, o_ref, h_ref):
    @pl.when(pl.program_id(1) == 0)
    def _():
        x = x_ref[...]
        ms = jnp.mean(x * x, axis=-1, keepdims=True)
        h_ref[...] = (x * lax.rsqrt(ms + EPS) * g_ref[...]).astype(BF16)

    o_ref[...] = jnp.dot(h_ref[...], w_ref[...], preferred_element_type=F32).astype(o_ref.dtype)


def _norm_matmul(x, g, w, out_dtype, tm, tn, name):
    T, D = x.shape
    N = w.shape[1]
    return pl.pallas_call(
        _norm_matmul_kernel,
        grid=(T // tm, N // tn),
        in_specs=[pl.BlockSpec((tm, D), lambda i, j: (i, 0)),
                  pl.BlockSpec((1, D), lambda i, j: (0, 0)),
                  pl.BlockSpec((D, tn), lambda i, j: (0, j))],
        out_specs=pl.BlockSpec((tm, tn), lambda i, j: (i, j)),
        out_shape=jax.ShapeDtypeStruct((T, N), out_dtype),
        scratch_shapes=[pltpu.VMEM((tm, D), BF16)],
        compiler_params=_cparams(("parallel", "arbitrary")),
        name=name,
    )(x, g, w)


def _row_cumsum(x):
    n = x.shape[0]
    row = lax.broadcasted_iota(I32, x.shape, 0)
    d = 1
    while d < n:
        x = x + jnp.where(row >= d, pltpu.roll(x, d, axis=0), 0.0)
        d *= 2
    return x


def _mlstm_kernel(qk_ref, v_ref, o_ref, misc_ref, cw_ref, cb_ref, gb_ref, gn_ref, out_ref,
                  xs_ref, ct_ref, n_ref, m_ref, *, heads, dh):
    first = pl.program_id(1) == 0

    @pl.when(first)
    def _():
        ct_ref[...] = jnp.zeros_like(ct_ref)
        n_ref[...] = jnp.zeros_like(n_ref)
        m_ref[...] = jnp.zeros_like(m_ref)

    L = qk_ref.shape[0]
    hd = heads * dh
    qk = _causal_conv(xs_ref, qk_ref[...].astype(F32), cw_ref, cb_ref, first, M_CONV)
    qk = qk * _sigmoid(qk)

    pre = misc_ref[...] + gb_ref[...]
    g_all = _row_cumsum(_log_sigmoid(pre))
    g_t = g_all.T
    pre_t = pre.T
    row = lax.broadcasted_iota(I32, (L, L), 0)
    col = lax.broadcasted_iota(I32, (L, L), 1)
    causal = row >= col

    for h in range(heads):
        q = qk[:, h * dh:(h + 1) * dh]
        k = qk[:, hd + h * dh:hd + (h + 1) * dh] * (dh ** -0.5)
        v = v_ref[:, h * dh:(h + 1) * dh]
        g_c = g_all[:, MISC_F + h:MISC_F + h + 1]
        i_c = pre[:, MISC_I + h:MISC_I + h + 1]
        g_r = g_t[MISC_F + h:MISC_F + h + 1, :]
        i_r = pre_t[MISC_I + h:MISC_I + h + 1, :]
        m_prev = m_ref[h:h + 1, 0:1]

        logd = jnp.where(causal, g_c - g_r + i_r, -jnp.inf)
        log_inter = g_c + m_prev
        m_t = jnp.maximum(log_inter, jnp.max(logd, axis=-1, keepdims=True))
        qb = q.astype(BF16)
        kb = k.astype(BF16)
        s = lax.dot_general(qb, kb, (((1,), (1,)), ((), ())), preferred_element_type=F32)
        s = s * jnp.exp(logd - m_t)
        w_inter = jnp.exp(log_inter - m_t)
        num = jnp.dot(s.astype(BF16), v, preferred_element_type=F32)
        num = num + w_inter * jnp.dot(qb, ct_ref[h].astype(BF16), preferred_element_type=F32)
        den = jnp.sum(s, axis=-1, keepdims=True)
        den = den + w_inter * jnp.sum(q * n_ref[h:h + 1, :], axis=-1, keepdims=True)
        hh = num / jnp.maximum(jnp.abs(den), jnp.exp(-m_t))

        g_last = g_c[L - 1:L, :]
        m_new = m_t[L - 1:L, :]
        w_state = jnp.exp(g_last - g_c + i_c - m_new)
        decay = jnp.exp(g_last + m_prev - m_new)
        kw = k * w_state
        ct_ref[h] = decay * ct_ref[h] + lax.dot_general(
            kw.astype(BF16), v, (((0,), (0,)), ((), ())), preferred_element_type=F32)
        n_ref[h:h + 1, :] = decay * n_ref[h:h + 1, :] + jnp.sum(kw, axis=0, keepdims=True)
        m_ref[h:h + 1, :] = jnp.broadcast_to(m_new, (1, LANES))

        mu = jnp.mean(hh, axis=-1, keepdims=True)
        hc = hh - mu
        var = jnp.mean(hc * hc, axis=-1, keepdims=True)
        y = hc * lax.rsqrt(var + EPS) * gn_ref[:, h * dh:(h + 1) * dh]
        y = y * _sigmoid(o_ref[:, h * dh:(h + 1) * dh].astype(F32))
        out_ref[:, h * dh:(h + 1) * dh] = y.astype(out_ref.dtype)


def _mlstm(proj, misc, conv_w, conv_b, gate_bias, norm_g, B, S, L):
    T = B * S
    nc = S // L
    hd = norm_g.shape[1]
    dh = hd // M_HEADS
    kern = functools.partial(_mlstm_kernel, heads=M_HEADS, dh=dh)
    return pl.pallas_call(
        kern,
        grid=(B, nc),
        in_specs=[pl.BlockSpec((L, 2 * hd), lambda b, c: (b * nc + c, 0)),
                  pl.BlockSpec((L, hd), lambda b, c: (b * nc + c, 2)),
                  pl.BlockSpec((L, hd), lambda b, c: (b * nc + c, 3)),
                  pl.BlockSpec((L, LANES), lambda b, c: (b * nc + c, 0)),
                  pl.BlockSpec((M_CONV, 2 * hd), lambda b, c: (0, 0)),
                  pl.BlockSpec((1, 2 * hd), lambda b, c: (0, 0)),
                  pl.BlockSpec((1, LANES), lambda b, c: (0, 0)),
                  pl.BlockSpec((1, hd), lambda b, c: (0, 0))],
        out_specs=pl.BlockSpec((L, hd), lambda b, c: (b * nc + c, 0)),
        out_shape=jax.ShapeDtypeStruct((T, hd), BF16),
        scratch_shapes=[pltpu.VMEM((L + SUBLANES, 2 * hd), F32),
                        pltpu.VMEM((M_HEADS, dh, dh), F32),
                        pltpu.VMEM((SUBLANES, dh), F32),
                        pltpu.VMEM((SUBLANES, LANES), F32)],
        compiler_params=_cparams(("parallel", "arbitrary")),
        name="mlstm",
    )(proj, proj, proj, misc, conv_w, conv_b, gate_bias, norm_g)


def _rglru_kernel(x_ref, g_ref, cw_ref, cb_ref, wa_ref, ba_ref, wx_ref, bx_ref, lam_ref, out_ref,
                  xs_ref, h_ref):
    first = pl.program_id(1) == 0

    @pl.when(first)
    def _():
        h_ref[...] = jnp.zeros_like(h_ref)

    tm = x_ref.shape[0]
    x = _causal_conv(xs_ref, x_ref[...].astype(F32), cw_ref, cb_ref, first, R_CONV)
    xb = x.astype(BF16)
    r = _sigmoid(jnp.dot(xb, wa_ref[...], preferred_element_type=F32) + ba_ref[...])
    i = _sigmoid(jnp.dot(xb, wx_ref[...], preferred_element_type=F32) + bx_ref[...])
    lam = lam_ref[...]
    softplus_neg_lam = jnp.maximum(-lam, 0.0) + jnp.log1p(jnp.exp(-jnp.abs(lam)))
    log_a = -R_C * r * softplus_neg_lam
    a = jnp.exp(log_a)
    u = jnp.sqrt(-_expm1(2.0 * log_a)) * (i * x)

    row = lax.broadcasted_iota(I32, a.shape, 0)
    d = 1
    while d < tm:
        valid = row >= d
        u = jnp.where(valid, a * pltpu.roll(u, d, axis=0) + u, u)
        a = jnp.where(valid, a * pltpu.roll(a, d, axis=0), a)
        d *= 2
    hcur = a * h_ref[0:1, :] + u
    h_ref[0:1, :] = hcur[tm - 1:tm, :]
    out_ref[...] = (_gelu(g_ref[...].astype(F32)) * hcur).astype(out_ref.dtype)


def _rglru(proj, conv_w, conv_b, wa, ba, wx, bx, lam, B, S, tm):
    T = B * S
    ns = S // tm
    C = lam.shape[1]
    vec = pl.BlockSpec((1, C), lambda b, s: (0, 0))
    mat = pl.BlockSpec((C, C), lambda b, s: (0, 0))
    return pl.pallas_call(
        _rglru_kernel,
        grid=(B, ns),
        in_specs=[pl.BlockSpec((tm, C), lambda b, s: (b * ns + s, 4)),
                  pl.BlockSpec((tm, C), lambda b, s: (b * ns + s, 5)),
                  pl.BlockSpec((R_CONV, C), lambda b, s: (0, 0)), vec,
                  mat, vec, mat, vec, vec],
        out_specs=pl.BlockSpec((tm, C), lambda b, s: (b * ns + s, 0)),
        out_shape=jax.ShapeDtypeStruct((T, C), BF16),
        scratch_shapes=[pltpu.VMEM((tm + SUBLANES, C), F32),
                        pltpu.VMEM((SUBLANES, C), F32)],
        compiler_params=_cparams(("parallel", "arbitrary")),
        name="rglru",
    )(proj, proj, conv_w, conv_b, wa, ba, wx, bx, lam)


def _rope_slab(x, cos, sin_signed, first_half):
    fwd = pltpu.roll(x, 32, axis=1)
    bwd = pltpu.roll(x, 96, axis=1)
    return x * cos + jnp.where(first_half, bwd, fwd) * sin_signed


def _dsa_prep_kernel(q_ref, k_ref, qi_ref, v_ref, misc_ref, cos_ref, sin_ref,
                     qo_ref, ko_ref, qio_ref, kio_ref, vt_ref, *, q_scale, idx_scale, head_dim):
    kb = vt_ref.shape[2]
    heads = v_ref.shape[1] // head_dim
    vrows = vt_ref.shape[1] // heads
    ones = jnp.ones((vrows - head_dim, kb), BF16)
    for c in range(vt_ref.shape[0]):
        vt = v_ref[c * kb:(c + 1) * kb, :].astype(F32).T.astype(BF16)
        for h in range(heads):
            vt_ref[c, h * vrows:h * vrows + head_dim, :] = vt[h * head_dim:(h + 1) * head_dim, :]
            vt_ref[c, h * vrows + head_dim:(h + 1) * vrows, :] = ones
    tm = q_ref.shape[0]
    cos = cos_ref[...]
    sin = sin_ref[...]
    lane = lax.broadcasted_iota(I32, (tm, LANES), 1)
    first_half = (lane % head_dim) < (head_dim // 2)
    low = lane < head_dim
    misc = misc_ref[...]
    nslab = q_ref.shape[1] // LANES
    for s in range(nslab):
        sl = slice(s * LANES, (s + 1) * LANES)
        qo_ref[:, sl] = (_rope_slab(q_ref[:, sl].astype(F32), cos, sin, first_half) * q_scale).astype(BF16)
        ko_ref[:, sl] = _rope_slab(k_ref[:, sl].astype(F32), cos, sin, first_half).astype(BF16)
        w_even = jnp.abs(misc[:, MISC_W + 2 * s:MISC_W + 2 * s + 1])
        w_odd = jnp.abs(misc[:, MISC_W + 2 * s + 1:MISC_W + 2 * s + 2])
        w = jnp.where(low, w_even, w_odd) * idx_scale
        qio_ref[:, sl] = (_rope_slab(qi_ref[:, sl].astype(F32), cos, sin, first_half) * w).astype(BF16)
    ki = jnp.where(low, _rope_slab(misc, cos, sin, first_half), 0.0)
    kio_ref[:, 0:LANES] = ki.astype(BF16)
    kio_ref[:, LANES:2 * LANES] = pltpu.roll(ki, head_dim, axis=1).astype(BF16)


def _dsa_prep(proj, misc, cos_t, sin_t, tm, kb):
    T = proj.shape[0]
    C = 512
    head_dim = C // A_HEADS
    vt_rows = A_HEADS * (head_dim + BF16_ROWS)
    kern = functools.partial(
        _dsa_prep_kernel, head_dim=head_dim,
        q_scale=float(head_dim ** -0.5 * np.log2(np.e)),
        idx_scale=float(head_dim ** -0.5 * IDX_HEADS ** -0.5))
    row = lambda i: (i, 0)
    return pl.pallas_call(
        kern,
        grid=(T // tm,),
        in_specs=[pl.BlockSpec((tm, C), lambda i: (i, 6)),
                  pl.BlockSpec((tm, C), lambda i: (i, 7)),
                  pl.BlockSpec((tm, C), lambda i: (i, 9)),
                  pl.BlockSpec((tm, C), lambda i: (i, 8)),
                  pl.BlockSpec((tm, LANES), row),
                  pl.BlockSpec((tm, LANES), row),
                  pl.BlockSpec((tm, LANES), row)],
        out_specs=[pl.BlockSpec((tm, C), row), pl.BlockSpec((tm, C), row),
                   pl.BlockSpec((tm, C), row), pl.BlockSpec((tm, 2 * LANES), row),
                   pl.BlockSpec((tm // kb, vt_rows, kb), lambda i: (i, 0, 0))],
        out_shape=[jax.ShapeDtypeStruct((T, C), BF16), jax.ShapeDtypeStruct((T, C), BF16),
                   jax.ShapeDtypeStruct((T, C), BF16), jax.ShapeDtypeStruct((T, 2 * LANES), BF16),
                   jax.ShapeDtypeStruct((T // kb, vt_rows, kb), BF16)],
        compiler_params=_cparams(("parallel",)),
        name="dsa_prep",
    )(proj, proj, proj, proj, misc, cos_t, sin_t)


def _bit_transpose(words):
    a = list(words)
    j, m = 16, 0x0000FFFF
    while j:
        for k in range(KEY_BITS):
            if k & j == 0:
                t = (a[k] ^ (a[k + j] >> j)) & m
                a[k] = a[k] ^ t
                a[k + j] = a[k + j] ^ (t << j)
        j >>= 1
        m = (m ^ (m << j)) & 0xFFFFFFFF
    return a


def _dsa_kernel(qi_ref, q_ref, misc_ref, ki_ref, k_ref, vt_ref, out_ref,
                key_ref, plane_ref, alive_ref, qm_ref, s_ref, acc_ref, m_ref, l_ref, *, top_k, kb, head_dim):
    Qb = q_ref.shape[0]
    heads = q_ref.shape[1] // head_dim
    qblk = pl.program_id(1)
    q0 = qblk * Qb
    nkb = (q0 + Qb + kb - 1) // kb
    ngrp = kb // SUBLANES
    nt = (((1,), (1,)), ((), ()))

    lane = lax.broadcasted_iota(I32, (Qb, LANES), 1)
    low = lane < head_dim
    for h in range(heads):
        sl = slice((h // 2) * LANES, (h // 2 + 1) * LANES)
        mine = low if h % 2 == 0 else jnp.logical_not(low)
        qm_ref[h] = jnp.where(mine, qi_ref[:, sl], jnp.zeros((), BF16))
        qm_ref[heads + h] = jnp.where(mine, q_ref[:, sl], jnp.zeros((), BF16))
    w_t = misc_ref[...].T
    sgn_t = jnp.where(w_t > 0, 1.0, jnp.where(w_t < 0, -1.0, 0.0))

    qpos = q0 + lax.broadcasted_iota(I32, (kb, Qb), 1)
    krow = lax.broadcasted_iota(I32, (kb, Qb), 0)
    sub = lax.broadcasted_iota(I32, (SUBLANES, Qb), 0)

    def score_block(j, carry):
        ki = ki_ref[pl.ds(pl.multiple_of(j * kb, kb), kb), :]
        for hp in range(heads // 2):
            part = None
            for h in (2 * hp, 2 * hp + 1):
                kih = ki[:, (h % 2) * LANES:(h % 2 + 1) * LANES]
                logit = lax.dot_general(kih, qm_ref[h], nt, preferred_element_type=F32)
                term = jnp.maximum(logit, 0.0) * sgn_t[MISC_W + h:MISC_W + h + 1, :]
                part = term if part is None else part + term
            if hp == 0:
                s_ref[0] = part
            else:
                s_ref[0] += part
        bits = pltpu.bitcast(s_ref[0], I32)
        bits = jnp.where(bits == INT_MIN, 0, bits)
        skey = bits ^ ((bits >> 31) & 0x7FFFFFFF)
        key_ref[j] = jnp.where(krow + j * kb <= qpos, skey, INT_MIN)
        for half in range(Qb // LANES):
            cols = slice(half * LANES, (half + 1) * LANES)
            words = [key_ref[j, i * SUBLANES:(i + 1) * SUBLANES, cols] ^ INT_MIN for i in range(KEY_BITS)]
            for t, plane in enumerate(_bit_transpose(words)):
                plane_ref[j, t, :, cols] = plane
        alive_ref[j] = jnp.full((SUBLANES, Qb), -1, I32)
        return carry

    lax.fori_loop(0, nkb, score_block, 0)

    @pl.when(nkb % 2 == 1)
    def _():
        alive_ref[nkb] = jnp.zeros((SUBLANES, Qb), I32)
        plane_ref[nkb] = jnp.zeros((KEY_BITS, SUBLANES, Qb), I32)

    npair = (nkb + 1) // 2

    def plane_step(t, above, tu, flip, first):
        def body(i, acc):
            for j in (2 * i, 2 * i + 1):
                alive = alive_ref[j]
                if not first:
                    alive = alive & (plane_ref[j, t - 1] ^ flip)
                    alive_ref[j] = alive
                acc = acc + lax.population_count(alive & plane_ref[j, t])
            return acc
        ones = lax.fori_loop(0, npair, body, jnp.zeros((SUBLANES, Qb), I32))
        ones = jnp.sum(ones, axis=0, keepdims=True)
        take = above + ones >= top_k
        tu = tu | jnp.where(take, lax.shift_left(jnp.int32(1), KEY_BITS - 1 - t), 0)
        above = jnp.where(take, above, above + ones)
        flip = jnp.broadcast_to(jnp.where(take, 0, -1), (SUBLANES, Qb))
        return above, tu, flip

    zero_row = jnp.zeros((1, Qb), I32)
    st = plane_step(0, zero_row, zero_row, None, True)
    above, tu, flip = lax.fori_loop(1, KEY_BITS, lambda t, st: plane_step(t, *st, False), st)

    def tie_body(j, acc):
        return acc + lax.population_count(alive_ref[j] & (plane_ref[j, KEY_BITS - 1] ^ flip))
    ties = lax.fori_loop(0, nkb, tie_body, jnp.zeros((SUBLANES, Qb), I32))
    c_thr = above + jnp.sum(ties, axis=0, keepdims=True)
    thr = jnp.maximum(tu ^ INT_MIN, INT_MIN + 1)

    def count(pred):
        def body(j, cnts):
            cnts = list(cnts)
            for r in range(ngrp):
                grp = key_ref[j, r * SUBLANES:(r + 1) * SUBLANES, :]
                a = r % COUNT_CHAINS
                cnts[a] = cnts[a] + jnp.where(pred(grp, j * kb + r * SUBLANES), 1, 0)
            return tuple(cnts)
        zero = jnp.zeros((SUBLANES, Qb), I32)
        cnts = lax.fori_loop(0, nkb, body, (zero,) * COUNT_CHAINS)
        return jnp.sum(functools.reduce(lambda x, y: x + y, cnts), axis=0, keepdims=True)

    def count_ge(thr):
        thr_b = jnp.broadcast_to(thr, (SUBLANES, Qb))
        return count(lambda grp, base: grp >= thr_b)

    excess = jnp.logical_and(c_thr > top_k, tu != 0)
    thr_b = jnp.broadcast_to(thr, (SUBLANES, Qb))

    @pl.when(jnp.max(excess.astype(I32)) > 0)
    def _():
        need = top_k - count_ge(thr + 1)

        def count_eq_before(pos):
            pos_b = jnp.broadcast_to(pos, (SUBLANES, Qb))
            return count(lambda grp, base: jnp.logical_and(grp == thr_b, base + sub < pos_b))

        def pos_body(t, pos):
            cand = pos | lax.shift_left(jnp.int32(1), 14 - t)
            return jnp.where(count_eq_before(cand) <= need, cand, pos)

        cut = lax.fori_loop(0, 15, pos_body, jnp.zeros((1, Qb), I32))
        cut_b = jnp.broadcast_to(jnp.where(excess, cut, jnp.int32(2 ** 30)), (SUBLANES, Qb))

        def demote(j, carry):
            for r in range(ngrp):
                rows = slice(r * SUBLANES, (r + 1) * SUBLANES)
                grp = key_ref[j, rows, :]
                hit = jnp.logical_and(grp == thr_b, j * kb + r * SUBLANES + sub >= cut_b)
                key_ref[j, rows, :] = jnp.where(hit, grp - 1, grp)
            return carry

        lax.fori_loop(0, nkb, demote, 0)

    acc_ref[...] = jnp.zeros_like(acc_ref)
    m_ref[...] = jnp.full(m_ref.shape, NEG_BIG, F32)
    l_ref[...] = jnp.zeros_like(l_ref)
    thr_full = jnp.broadcast_to(thr, (kb, Qb))
    vrows = vt_ref.shape[1] // heads

    def attend_block(j, carry):
        start = pl.multiple_of(j * kb, kb)
        bias = jnp.where(key_ref[j] >= thr_full, 0.0, -jnp.inf)
        m_blk = []
        for h in range(heads):
            sl = slice((h // 2) * LANES, (h // 2 + 1) * LANES)
            s = lax.dot_general(k_ref[pl.ds(start, kb), sl], qm_ref[heads + h], nt,
                                preferred_element_type=F32) + bias
            s_ref[h] = s
            m_blk.append(jnp.max(s, axis=0, keepdims=True))
        for h in range(heads):
            dims = slice(h * head_dim, (h + 1) * head_dim)
            m_old = m_ref[h]
            m_new = jnp.maximum(m_old, m_blk[h])
            p = jnp.exp2(s_ref[h] - m_new)
            alpha = jnp.exp2(m_old - m_new)
            m_ref[h] = m_new
            pv = jnp.dot(vt_ref[j, h * vrows:(h + 1) * vrows, :], p.astype(BF16),
                         preferred_element_type=F32)
            acc_ref[dims, :] = alpha * acc_ref[dims, :] + pv[0:head_dim, :]
            l_ref[h] = alpha * l_ref[h] + pv[head_dim:head_dim + 1, :]
        return carry

    lax.fori_loop(0, nkb, attend_block, 0)

    for h in range(heads):
        dims = slice(h * head_dim, (h + 1) * head_dim)
        acc_ref[dims, :] = acc_ref[dims, :] / l_ref[h]
    out_ref[...] = acc_ref[...].T.astype(out_ref.dtype)


def _dsa(qi_r, q_r, misc, ki_r, k_r, v_t, B, S, Qb, kb):
    T = B * S
    C = q_r.shape[1]
    nq = S // Qb
    top_k = min(TOPK_MAX, S // 4)
    assert kb == KEY_BITS * SUBLANES, "one bit-plane word packs one key per sublane group of a key block"
    kern = functools.partial(_dsa_kernel, top_k=top_k, kb=kb, head_dim=C // A_HEADS)
    once = pl.Buffered(1)
    return pl.pallas_call(
        kern,
        grid=(B, nq),
        in_specs=[pl.BlockSpec((Qb, C), lambda b, i: (b * nq + i, 0)),
                  pl.BlockSpec((Qb, C), lambda b, i: (b * nq + i, 0)),
                  pl.BlockSpec((Qb, LANES), lambda b, i: (b * nq + i, 0)),
                  pl.BlockSpec((S, 2 * LANES), lambda b, i: (b, 0), pipeline_mode=once),
                  pl.BlockSpec((S, C), lambda b, i: (b, 0), pipeline_mode=once),
                  pl.BlockSpec((S // kb, v_t.shape[1], kb), lambda b, i: (b, 0, 0), pipeline_mode=once)],
        out_specs=pl.BlockSpec((Qb, C), lambda b, i: (b * nq + i, 0)),
        out_shape=jax.ShapeDtypeStruct((T, C), BF16),
        scratch_shapes=[pltpu.VMEM((S // kb, kb, Qb), I32),
                        pltpu.VMEM((S // kb, KEY_BITS, SUBLANES, Qb), I32),
                        pltpu.VMEM((S // kb, SUBLANES, Qb), I32),
                        pltpu.VMEM((2 * A_HEADS, Qb, LANES), BF16),
                        pltpu.VMEM((A_HEADS, kb, Qb), F32),
                        pltpu.VMEM((C, Qb), F32),
                        pltpu.VMEM((A_HEADS, 1, Qb), F32),
                        pltpu.VMEM((A_HEADS, 1, Qb), F32)],
        compiler_params=_cparams(("parallel", "arbitrary")),
        name="dsa",
    )(qi_r, q_r, misc, ki_r, k_r, v_t)


def _merge_kernel(ya_ref, yb_ref, yc_ref, ga_ref, gb_ref, gc_ref, x_ref, wb_ref, wo_ref, out_ref):
    def branch(y_ref, g_ref, n):
        return _sigmoid(g_ref[...].astype(F32)) * jnp.dot(y_ref[...], wb_ref[n], preferred_element_type=F32)

    merged = branch(ya_ref, ga_ref, 0) + branch(yb_ref, gb_ref, 1) + branch(yc_ref, gc_ref, 2)
    out_ref[...] = x_ref[...] + jnp.dot(merged.astype(BF16), wo_ref[...], preferred_element_type=F32)


def _merge(ya, yb, yc, proj, x, wb, wo, tm):
    T, D = x.shape
    C = ya.shape[1]
    row = lambda i: (i, 0)
    gate0 = 5120 // D
    return pl.pallas_call(
        _merge_kernel,
        grid=(T // tm,),
        in_specs=[pl.BlockSpec((tm, C), row), pl.BlockSpec((tm, C), row), pl.BlockSpec((tm, C), row),
                  pl.BlockSpec((tm, D), lambda i: (i, gate0)),
                  pl.BlockSpec((tm, D), lambda i: (i, gate0 + 1)),
                  pl.BlockSpec((tm, D), lambda i: (i, gate0 + 2)),
                  pl.BlockSpec((tm, D), row),
                  pl.BlockSpec((3, C, D), lambda i: (0, 0, 0)),
                  pl.BlockSpec((D, D), lambda i: (0, 0))],
        out_specs=pl.BlockSpec((tm, D), row),
        out_shape=jax.ShapeDtypeStruct((T, D), F32),
        compiler_params=_cparams(("parallel",)),
        name="merge_out",
    )(ya, yb, yc, proj, proj, proj, x, wb, wo)


def _ffn_down_kernel(g_ref, u_ref, cw_ref, cb_ref, wd_ref, x_ref, out_ref, xs_ref):
    first = pl.program_id(1) == 0
    g = _causal_conv(xs_ref, g_ref[...].astype(F32), cw_ref, cb_ref, first, FFN_CONV)
    a = _gelu(g) * u_ref[...].astype(F32)
    out_ref[...] = x_ref[...] + jnp.dot(a.astype(BF16), wd_ref[...], preferred_element_type=F32)


def _ffn_down(up, conv_w, conv_b, wd, x, B, S, tm):
    T, D = x.shape
    F = wd.shape[0]
    ns = S // tm
    return pl.pallas_call(
        _ffn_down_kernel,
        grid=(B, ns),
        in_specs=[pl.BlockSpec((tm, F), lambda b, s: (b * ns + s, 0)),
                  pl.BlockSpec((tm, F), lambda b, s: (b * ns + s, 1)),
                  pl.BlockSpec((FFN_CONV, F), lambda b, s: (0, 0)),
                  pl.BlockSpec((1, F), lambda b, s: (0, 0)),
                  pl.BlockSpec((F, D), lambda b, s: (0, 0)),
                  pl.BlockSpec((tm, D), lambda b, s: (b * ns + s, 0))],
        out_specs=pl.BlockSpec((tm, D), lambda b, s: (b * ns + s, 0)),
        out_shape=jax.ShapeDtypeStruct((T, D), F32),
        scratch_shapes=[pltpu.VMEM((tm + SUBLANES, F), F32)],
        compiler_params=_cparams(("parallel", "arbitrary")),
        name="ffn_down",
    )(up, up, conv_w, conv_b, wd, x)


def _final_norm_kernel(x_ref, g_ref, o_ref):
    x = x_ref[...]
    ms = jnp.mean(x * x, axis=-1, keepdims=True)
    o_ref[...] = x * lax.rsqrt(ms + EPS) * g_ref[...]


def _final_norm(x, g, tm):
    T, D = x.shape
    return pl.pallas_call(
        _final_norm_kernel,
        grid=(T // tm,),
        in_specs=[pl.BlockSpec((tm, D), lambda i: (i, 0)), pl.BlockSpec((1, D), lambda i: (0, 0))],
        out_specs=pl.BlockSpec((tm, D), lambda i: (i, 0)),
        out_shape=jax.ShapeDtypeStruct((T, D), F32),
        compiler_params=_cparams(("parallel",)),
        name="final_norm",
    )(x, g)


def _block_diag(w):
    n, d, e = w.shape
    eye = jnp.eye(n, dtype=w.dtype)
    return (w[:, :, None, :] * eye[:, None, :, None]).reshape(n * d, n * e)


def _split_in_proj(w_in, d_branch):
    sizes = (2 * d_branch, d_branch, d_branch, M_HEADS, M_HEADS, d_branch, d_branch, d_branch, d_branch,
             d_branch, d_branch, d_branch // IDX_HEADS, IDX_HEADS, 3 * w_in.shape[0])
    offs = np.cumsum((0,) + sizes)
    assert offs[-1] == w_in.shape[1]
    (m_qk, m_v, m_o, m_i, m_f, r_x, r_g, a_q, a_k, a_v, a_qi, a_ki, a_w, gates) = [
        w_in[:, offs[n]:offs[n + 1]] for n in range(len(sizes))]
    main = jnp.concatenate([m_qk, m_v, m_o, r_x, r_g, a_q, a_k, a_v, a_qi, gates], axis=1).astype(BF16)
    pad = jnp.zeros((w_in.shape[0], LANES - MISC_W - IDX_HEADS), w_in.dtype)
    side = jnp.concatenate([a_ki, m_i, m_f, a_w, pad], axis=1).astype(BF16)
    return main, side


def kernel(x, positions, norm_mix, w_in, mlstm_conv_w, mlstm_conv_b, mlstm_i_bias, mlstm_f_bias, mlstm_norm, rglru_conv_w, rglru_conv_b, rglru_w_a, rglru_b_a, rglru_w_x, rglru_b_x, rglru_lambda, w_branch, w_out, norm_ffn, ffn_up, ffn_conv_w, ffn_conv_b, ffn_down, norm_final):
    B, S, D = x.shape
    T = B * S
    depth = w_in.shape[0]
    d_branch = mlstm_norm.shape[1]
    d_ff = ffn_down.shape[1]
    assert D == 1024 and d_branch == 512 and d_ff == 3 * D, "column-block indexing assumes these widths"

    tm = min(1024, S)
    chunk = min(128, S)
    q_block = min(256, S)
    k_block = min(256, S)

    cos_t, sin_t = _rope_tables(positions, d_branch // A_HEADS)
    xf = x.reshape(T, D)
    for l in range(depth):
        w_main, w_side = _split_in_proj(w_in[l], d_branch)
        g_mix = norm_mix[l][None, :]
        proj = _norm_matmul(xf, g_mix, w_main, BF16, min(2 * tm, T), 1024, "in_proj")
        misc = _norm_matmul(xf, g_mix, w_side, F32, tm, LANES, "in_proj_side")

        gate_bias = jnp.zeros((1, LANES), F32)
        gate_bias = gate_bias.at[0, MISC_I:MISC_I + M_HEADS].set(mlstm_i_bias[l])
        gate_bias = gate_bias.at[0, MISC_F:MISC_F + M_HEADS].set(mlstm_f_bias[l])
        y_a = _mlstm(proj, misc, mlstm_conv_w[l], mlstm_conv_b[l][None, :], gate_bias,
                     mlstm_norm[l][None, :], B, S, chunk)

        y_b = _rglru(proj, rglru_conv_w[l], rglru_conv_b[l][None, :],
                     _block_diag(rglru_w_a[l]).astype(BF16), rglru_b_a[l][None, :],
                     _block_diag(rglru_w_x[l]).astype(BF16), rglru_b_x[l][None, :],
                     rglru_lambda[l][None, :], B, S, min(256, S))

        q_r, k_r, qi_r, ki_r, v_t = _dsa_prep(proj, misc, cos_t, sin_t, tm, k_block)
        y_c = _dsa(qi_r, q_r, misc, ki_r, k_r, v_t, B, S, q_block, k_block)

        xf = _merge(y_a, y_b, y_c, proj, xf, w_branch[l].astype(BF16), w_out[l].astype(BF16), min(1024, S))

        up = _norm_matmul(xf, norm_ffn[l][None, :], ffn_up[l].astype(BF16), BF16, min(2 * tm, T), 1024, "ffn_up")
        xf = _ffn_down(up, ffn_conv_w[l], ffn_conv_b[l][None, :], ffn_down[l].astype(BF16), xf,
                       B, S, min(256, S))
    return _final_norm(xf, norm_final[None, :], tm).reshape(B, S, D)
```
